```python
import math
import jax, jax.numpy as jnp
from jax import lax
import numpy as np

D_MODEL = 4096
BATCH = 16
SEQ = 256
DEPTH = 2
DEC_BATCH = 2
DEC_SEQ = 1024
PAST_LEN = 512

GRID_W = 64
EPS = 1e-6
D_S5 = D_MODEL // 4
D_SSD = D_MODEL // 2
D_HY = D_MODEL - D_S5 - D_SSD
D_MIX = D_S5 + D_SSD + D_HY
S5_H = 16
S5_GROUPS = D_S5 // S5_H
S5_P = 64
SSD_HEADDIM = 64
SSD_HEADS = D_SSD // SSD_HEADDIM
SSD_NGROUPS = 4
SSD_HPG = SSD_HEADS // SSD_NGROUPS
SSD_DSTATE = 128
SSD_CHUNK = 128
SSD_CONV_DIM = D_SSD + 2 * SSD_NGROUPS * SSD_DSTATE
CONV_W = 3
HY_ORDER = 2
HY_BANDS = 16
HY_EMB = 1 + 2 * HY_BANDS
HY_FFN = 64
HY_MIN_DECAY = math.log(1e-2) / 1.5
HY_MAX_DECAY = math.log(1e-2) / 0.3
D_FF = 14336
N_EXPERTS = 8
TOP_K = 2
D_FF_EXPERT = 14336
N_DENSE = (DEPTH + 1) // 2
N_MOE = DEPTH // 2
OFF_Z = D_S5
OFF_XBC = OFF_Z + D_SSD
OFF_DT = OFF_XBC + SSD_CONV_DIM
OFF_HY = OFF_DT + 2 * SSD_HEADS
N_IN = OFF_HY + (HY_ORDER + 1) * D_HY

kernel_name = 'hybrid_s5_ssd_hyena_flow_trunk'


def rmsnorm(x, w):
    xf = x.astype(jnp.float32)
    y = xf * lax.rsqrt(jnp.mean(xf * xf, axis=-1, keepdims=True) + EPS)
    return (y * w.astype(jnp.float32)).astype(x.dtype)


def centred_conv(u, w, b, grid):
    bsz, L, C = u.shape
    n_seg, seg = grid
    us = u.reshape(bsz, n_seg, seg, C)
    pad = CONV_W // 2
    up = jnp.pad(us, ((0, 0), (0, 0), (pad, pad), (0, 0)))
    y = b + sum(w[k] * up[:, :, k:k + seg] for k in range(CONV_W))
    return y.reshape(bsz, L, C)


def cmul(ar, ai, br, bi):
    return ar * br - ai * bi, ar * bi + ai * br


def s5_scan(bu_re, bu_im, ab_re, ab_im, x0_re, x0_im):
    i_re, i_im = cmul(ab_re, ab_im, x0_re, x0_im)
    bu_re = bu_re.at[:, 0].add(i_re)
    bu_im = bu_im.at[:, 0].add(i_im)
    a_re = jnp.broadcast_to(ab_re, bu_re.shape)
    a_im = jnp.broadcast_to(ab_im, bu_im.shape)

    def combine(e1, e2):
        a1r, a1i, b1r, b1i = e1
        a2r, a2i, b2r, b2i = e2
        ar, ai = cmul(a2r, a2i, a1r, a1i)
        br, bi = cmul(a2r, a2i, b1r, b1i)
        return ar, ai, br + b2r, bi + b2i

    _, _, x_re, x_im = lax.associative_scan(combine, (a_re, a_im, bu_re, bu_im), axis=1)
    return x_re, x_im


def s5_direction(u, bb_re, bb_im, ab_re, ab_im, x0_re, x0_im):
    bu_re = jnp.einsum('blgh,gph->blgp', u, bb_re)
    bu_im = jnp.einsum('blgh,gph->blgp', u, bb_im)
    return s5_scan(bu_re, bu_im, ab_re, ab_im, x0_re, x0_im)


def s5_mixer(u, a_re, a_im, log_dt, b_re, b_im, c_re, c_im, d, w_glu, x0_re, x0_im):
    f32 = jnp.float32
    bsz, L, _ = u.shape
    uf = u.astype(f32).reshape(bsz, L, S5_GROUPS, S5_H)
    ar, ai = a_re.astype(f32), a_im.astype(f32)
    dt = jnp.exp(log_dt.astype(f32))[..., None]
    mag = jnp.exp(ar * dt)
    ab_re, ab_im = mag * jnp.cos(ai * dt), mag * jnp.sin(ai * dt)
    den = ar * ar + ai * ai
    q_re = ((ab_re - 1.0) * ar + ab_im * ai) / den
    q_im = (ab_im * ar - (ab_re - 1.0) * ai) / den
    bb_re, bb_im = cmul(q_re[..., None], q_im[..., None], b_re.astype(f32)[None], b_im.astype(f32)[None])
    x0r, x0i = x0_re.astype(f32), x0_im.astype(f32)
    xf_re, xf_im = s5_direction(uf, bb_re[0], bb_im[0], ab_re[0], ab_im[0], x0r[:, 0], x0i[:, 0])
    xb_re, xb_im = s5_direction(jnp.flip(uf, axis=1), bb_re[1], bb_im[1], ab_re[1], ab_im[1], x0r[:, 1], x0i[:, 1])
    fin_re = jnp.stack([xf_re[:, -1], xb_re[:, -1]], axis=1)
    fin_im = jnp.stack([xf_im[:, -1], xb_im[:, -1]], axis=1)
    st_re = xf_re + jnp.flip(xb_re, axis=1)
    st_im = xf_im + jnp.flip(xb_im, axis=1)
    y = (jnp.einsum('blgp,ghp->blgh', st_re, c_re.astype(f32))
         - jnp.einsum('blgp,ghp->blgh', st_im, c_im.astype(f32))
         + d.astype(f32).reshape(S5_GROUPS, S5_H) * uf)
    y = jax.nn.gelu(y.reshape(bsz, L, D_S5))
    y = y * jax.nn.sigmoid(y @ w_glu.astype(f32))
    return y, fin_re, fin_im


def segsum(x):
    T = x.shape[-1]
    xr = jnp.broadcast_to(x[..., :, None], x.shape + (T,))
    xr = jnp.where(jnp.tril(jnp.ones((T, T), bool), -1), xr, 0.0)
    cs = jnp.cumsum(xr, axis=-2)
    return jnp.where(jnp.tril(jnp.ones((T, T), bool)), cs, -jnp.inf)


def ssd_chunked(x, dt, a, bm, cm, init):
    bsz, L = x.shape[:2]
    nc = L // SSD_CHUNK
    xs = (x * dt[..., None]).reshape(bsz, nc, SSD_CHUNK, SSD_NGROUPS, SSD_HPG, SSD_HEADDIM)
    adt = jnp.moveaxis((dt * a).reshape(bsz, nc, SSD_CHUNK, SSD_NGROUPS, SSD_HPG), 2, -1)
    bq = bm.reshape(bsz, nc, SSD_CHUNK, SSD_NGROUPS, SSD_DSTATE)
    cq = cm.reshape(bsz, nc, SSD_CHUNK, SSD_NGROUPS, SSD_DSTATE)
    a_cs = jnp.cumsum(adt, axis=-1)
    lmat = jnp.exp(segsum(adt))
    y_diag = jnp.einsum('bclgn,bcsgn,bcgrls,bcsgrp->bclgrp', cq, bq, lmat, xs)
    decay_states = jnp.exp(a_cs[..., -1:] - a_cs)
    states = jnp.einsum('bclgn,bcgrl,bclgrp->bcgrpn', bq, decay_states, xs)
    init_q = init.reshape(bsz, 1, SSD_NGROUPS, SSD_HPG, SSD_HEADDIM, SSD_DSTATE)
    states = jnp.concatenate([init_q, states], axis=1)
    chunk_tot = jnp.moveaxis(jnp.pad(a_cs[..., -1], ((0, 0), (1, 0), (0, 0), (0, 0))), 1, -1)
    decay_chunk = jnp.exp(segsum(chunk_tot))
    new_states = jnp.einsum('bgrzc,bcgrpn->bzgrpn', decay_chunk, states)
    states, final = new_states[:, :-1], new_states[:, -1]
    y_off = jnp.einsum('bclgn,bcgrpn,bcgrl->bclgrp', cq, states, jnp.exp(a_cs))
    y = (y_diag + y_off).reshape(bsz, L, SSD_HEADS, SSD_HEADDIM)
    return y, final.reshape(bsz, SSD_HEADS, SSD_HEADDIM, SSD_DSTATE)


def ssd_mixer(z, xbc, dt_raw, conv_w, conv_b, dt_bias, a_log, d, norm_w, init, grid):
    f32 = jnp.float32
    bsz, L, _ = z.shape
    xbc = jax.nn.silu(centred_conv(xbc, conv_w, conv_b, grid)).astype(f32)
    xs, bm, cm = jnp.split(xbc, [D_SSD, D_SSD + SSD_NGROUPS * SSD_DSTATE], axis=-1)
    x = xs.reshape(bsz, L, SSD_HEADS, SSD_HEADDIM)
    bm = bm.reshape(bsz, L, SSD_NGROUPS, SSD_DSTATE)
    cm = cm.reshape(bsz, L, SSD_NGROUPS, SSD_DSTATE)
    dt = jax.nn.softplus(dt_raw.astype(f32) + dt_bias.astype(f32))
    a = -jnp.exp(a_log.astype(f32))
    init = init.astype(f32)
    y_f, fin_f = ssd_chunked(x, dt[:, :, 0], a[0], bm, cm, init[:, 0])
    y_b, fin_b = ssd_chunked(jnp.flip(x, axis=1), jnp.flip(dt[:, :, 1], axis=1), a[1],
                             jnp.flip(bm, axis=1), jnp.flip(cm, axis=1), init[:, 1])
    y = y_f + jnp.flip(y_b, axis=1) + d.astype(f32)[:, None] * x
    y = y.reshape(bsz, L, D_SSD) * jax.nn.silu(z.astype(f32))
    return rmsnorm(y, norm_w), jnp.stack([fin_f, fin_b], axis=1)


def hyena_filters(L, w1, b1, w2, b2, w3, freq):
    f32 = jnp.float32
    t = jnp.arange(L, dtype=f32) / L
    bands = jnp.linspace(1e-4, HY_BANDS - 1, HY_BANDS, dtype=f32)
    ang = 2.0 * math.pi * t[:, None] * bands[None, :]
    feat = jnp.concatenate([t[:, None], jnp.cos(ang), -jnp.sin(ang)], axis=-1)
    hdn = jnp.sin(freq[0] * (feat @ w1 + b1))
    hdn = jnp.sin(freq[1] * (hdn @ w2 + b2))
    filt = (hdn @ w3).astype(f32).reshape(L, 2, HY_ORDER, D_HY)
    deltas = jnp.abs(jnp.linspace(HY_MIN_DECAY, HY_MAX_DECAY, D_HY, dtype=f32))
    filt = filt * jnp.exp(-t[:, None] * deltas[None, :])[:, None, None, :]
    hf, hb = filt[:, 0], filt[:, 1]
    return jnp.concatenate([hf[:1] + hb[:1], hf[1:], jnp.zeros_like(hf[:1]), jnp.flip(hb[1:], axis=0)], axis=0)


def hyena_mixer(u3, conv_w, conv_b, k, bias, grid):
    f32 = jnp.float32
    L = u3.shape[1]
    u3 = centred_conv(u3, conv_w, conv_b, grid).astype(f32)
    v, x1, x2 = jnp.split(u3, 3, axis=-1)
    kf = jnp.fft.rfft(k, n=2 * L, axis=0)
    zc = v
    for o, gate in enumerate((x1, x2)):
        zf = jnp.fft.rfft(zc, n=2 * L, axis=1)
        conv = jnp.fft.irfft(zf * kf[:, o], n=2 * L, axis=1)[:, :L]
        zc = gate * (conv + bias[o].astype(f32) * zc)
    return zc


def mixer(h, l, grid, s5_x0_re, s5_x0_im, ssd_x0, P):
    bsz, L, _ = h.shape
    proj = h @ P['w_in'][l]
    u_s5, z, xbc, dt_raw, u_hy = jnp.split(proj, [OFF_Z, OFF_XBC, OFF_DT, OFF_HY], axis=-1)
    y_s5, fs_re, fs_im = s5_mixer(u_s5, P['s5_a_re'][l], P['s5_a_im'][l], P['s5_log_dt'][l],
                                  P['s5_b_re'][l], P['s5_b_im'][l], P['s5_c_re'][l], P['s5_c_im'][l],
                                  P['s5_d'][l], P['s5_w_glu'][l], s5_x0_re, s5_x0_im)
    y_ssd, fs_ssd = ssd_mixer(z, xbc, dt_raw.reshape(bsz, L, 2, SSD_HEADS), P['ssd_conv_w'][l], P['ssd_conv_b'][l],
                              P['ssd_dt_bias'][l], P['ssd_a_log'][l], P['ssd_d'][l], P['ssd_norm'][l], ssd_x0, grid)
    k = hyena_filters(L, P['hy_w1'][l], P['hy_b1'][l], P['hy_w2'][l], P['hy_b2'][l], P['hy_w3'][l], P['hy_freq'][l])
    y_hy = hyena_mixer(u_hy, P['hy_conv_w'][l], P['hy_conv_b'][l], k, P['hy_bias'][l], grid)
    y = jnp.concatenate([y_s5.astype(h.dtype), y_ssd.astype(h.dtype), y_hy.astype(h.dtype)], axis=-1)
    return y @ P['w_out'][l], fs_re, fs_im, fs_ssd


def swiglu(t, wg, wu, wd):
    return (jax.nn.silu(t @ wg) * (t @ wu)) @ wd


def moe_swiglu(h, w_router, wg, wu, wd):
    bsz, L, D = h.shape
    t = h.reshape(bsz * L, D)
    logits = (t @ w_router).astype(jnp.float32)
    top_v, top_i = lax.top_k(logits, TOP_K)
    gate = jax.nn.softmax(top_v, axis=-1)
    combine = jnp.einsum('tk,tke->te', gate, jax.nn.one_hot(top_i, N_EXPERTS, dtype=jnp.float32))
    out = jnp.zeros((bsz * L, D), jnp.float32)
    for e in range(N_EXPERTS):
        out = out + combine[:, e:e + 1] * swiglu(t, wg[e], wu[e], wd[e])
    return out.reshape(bsz, L, D)


def trunk(x, cond, grid, s5_re0, s5_im0, ssd0, P):
    fin_re, fin_im, fin_ssd = [], [], []
    for l in range(DEPTH):
        mod = jax.nn.silu(cond.astype(jnp.float32)) @ P['ada_w'][l] + P['ada_b'][l]
        sh1, sc1, g1, sh2, sc2, g2 = jnp.split(mod[:, None, :], 6, axis=-1)
        h = rmsnorm(x, P['norm1'][l]) * (1.0 + sc1) + sh1
        mix, fs_re, fs_im, fs_ssd = mixer(h, l, grid, s5_re0[:, l], s5_im0[:, l], ssd0[:, l], P)
        x = x + g1 * mix
        h = rmsnorm(x, P['norm2'][l]) * (1.0 + sc2) + sh2
        if l % 2 == 0:
            f = swiglu(h, P['ffn_wg'][l // 2], P['ffn_wu'][l // 2], P['ffn_wd'][l // 2])
        else:
            f = moe_swiglu(h, P['moe_router'][l // 2], P['moe_wg'][l // 2], P['moe_wu'][l // 2], P['moe_wd'][l // 2])
        x = x + g2 * f
        fin_re.append(fs_re)
        fin_im.append(fs_im)
        fin_ssd.append(fs_ssd)
    y = rmsnorm(x, P['final_norm'])
    return y, jnp.stack(fin_re, axis=1), jnp.stack(fin_im, axis=1), jnp.stack(fin_ssd, axis=1)


def setup_inputs(seed: int = 0) -> dict:
    key = jax.random.key(seed)
    keys = list(jax.random.split(key, 64))
    f32 = jnp.float32

    def nrm(shape, scale):
        return jax.random.normal(keys.pop(), shape, f32) * scale

    def unif(shape, lo, hi):
        return jax.random.uniform(keys.pop(), shape, f32, lo, hi)

    def gain(shape):
        return 1.0 + nrm(shape, 0.02)

    s5_n = jnp.arange(S5_P, dtype=f32)
    ssd_dt = jnp.exp(unif((DEPTH, 2, SSD_HEADS), math.log(1e-3), math.log(1e-1)))
    return {
        'x_prompt': nrm((BATCH, SEQ, D_MODEL), 1.0),
        'x_sample': nrm((DEC_BATCH, DEC_SEQ, D_MODEL), 1.0),
        'state_s5_re': nrm((DEC_BATCH, DEPTH, 2, S5_GROUPS, S5_P), 0.3),
        'state_s5_im': nrm((DEC_BATCH, DEPTH, 2, S5_GROUPS, S5_P), 0.3),
        'state_ssd': nrm((DEC_BATCH, DEPTH, 2, SSD_HEADS, SSD_HEADDIM, SSD_DSTATE), 0.1),
        'c': nrm((DEC_BATCH, D_MODEL), 1.0),
        'c_ctx': nrm((D_MODEL,), 1.0),
        'ada_w': nrm((DEPTH, D_MODEL, 6 * D_MODEL), 0.5 * D_MODEL ** -0.5),
        'ada_b': nrm((DEPTH, 6 * D_MODEL), 0.02),
        'norm1': gain((DEPTH, D_MODEL)),
        'norm2': gain((DEPTH, D_MODEL)),
        'final_norm': gain((D_MODEL,)),
        'w_in': nrm((DEPTH, D_MODEL, N_IN), D_MODEL ** -0.5),
        'w_out': nrm((DEPTH, D_MIX, D_MODEL), D_MIX ** -0.5),
        's5_a_re': -0.5 + nrm((DEPTH, 2, S5_GROUPS, S5_P), 0.01),
        's5_a_im': math.pi * s5_n + nrm((DEPTH, 2, S5_GROUPS, S5_P), 0.01),
        's5_log_dt': unif((DEPTH, 2, S5_GROUPS), math.log(1e-3), math.log(1e-1)),
        's5_b_re': nrm((DEPTH, S5_GROUPS, S5_P, S5_H), (2 * S5_H) ** -0.5),
        's5_b_im': nrm((DEPTH, S5_GROUPS, S5_P, S5_H), (2 * S5_H) ** -0.5),
        's5_c_re': nrm((DEPTH, S5_GROUPS, S5_H, S5_P), (2 * S5_P) ** -0.5),
        's5_c_im': nrm((DEPTH, S5_GROUPS, S5_H, S5_P), (2 * S5_P) ** -0.5),
        's5_d': nrm((DEPTH, D_S5), 1.0),
        's5_w_glu': nrm((DEPTH, D_S5, D_S5), D_S5 ** -0.5),
        'ssd_conv_w': nrm((DEPTH, CONV_W, SSD_CONV_DIM), CONV_W ** -0.5),
        'ssd_conv_b': nrm((DEPTH, SSD_CONV_DIM), 0.02),
        'ssd_dt_bias': ssd_dt + jnp.log(-jnp.expm1(-ssd_dt)),
        'ssd_a_log': jnp.log(unif((DEPTH, 2, SSD_HEADS), 1.0, 16.0)),
        'ssd_d': gain((DEPTH, SSD_HEADS)),
        'ssd_norm': gain((DEPTH, D_SSD)),
        'hy_conv_w': nrm((DEPTH, CONV_W, 3 * D_HY), CONV_W ** -0.5),
        'hy_conv_b': nrm((DEPTH, 3 * D_HY), 0.02),
        'hy_w1': nrm((DEPTH, HY_EMB, HY_FFN), HY_EMB ** -0.5),
        'hy_b1': nrm((DEPTH, HY_FFN), 0.1),
        'hy_w2': nrm((DEPTH, HY_FFN, HY_FFN), HY_FFN ** -0.5),
        'hy_b2': nrm((DEPTH, HY_FFN), 0.1),
        'hy_w3': nrm((DEPTH, HY_FFN, 2 * HY_ORDER * D_HY), 0.1 * HY_FFN ** -0.5),
        'hy_freq': gain((DEPTH, 2, HY_FFN)),
        'hy_bias': nrm((DEPTH, HY_ORDER, D_HY), 0.5),
        'ffn_wg': nrm((N_DENSE, D_MODEL, D_FF), D_MODEL ** -0.5),
        'ffn_wu': nrm((N_DENSE, D_MODEL, D_FF), D_MODEL ** -0.5),
        'ffn_wd': nrm((N_DENSE, D_FF, D_MODEL), D_FF ** -0.5),
        'moe_router': nrm((N_MOE, D_MODEL, N_EXPERTS), D_MODEL ** -0.5),
        'moe_wg': nrm((N_MOE, N_EXPERTS, D_MODEL, D_FF_EXPERT), D_MODEL ** -0.5),
        'moe_wu': nrm((N_MOE, N_EXPERTS, D_MODEL, D_FF_EXPERT), D_MODEL ** -0.5),
        'moe_wd': nrm((N_MOE, N_EXPERTS, D_FF_EXPERT, D_MODEL), D_FF_EXPERT ** -0.5),
    }


def reference(x_prompt, x_sample, state_s5_re, state_s5_im, state_ssd, c, c_ctx,
              ada_w, ada_b, norm1, norm2, final_norm, w_in, w_out,
              s5_a_re, s5_a_im, s5_log_dt, s5_b_re, s5_b_im, s5_c_re, s5_c_im, s5_d, s5_w_glu,
              ssd_conv_w, ssd_conv_b, ssd_dt_bias, ssd_a_log, ssd_d, ssd_norm,
              hy_conv_w, hy_conv_b, hy_w1, hy_b1, hy_w2, hy_b2, hy_w3, hy_freq, hy_bias,
              ffn_wg, ffn_wu, ffn_wd, moe_router, moe_wg, moe_wu, moe_wd):
    P = {
        'ada_w': ada_w, 'ada_b': ada_b, 'norm1': norm1, 'norm2': norm2, 'final_norm': final_norm,
        'w_in': w_in, 'w_out': w_out,
        's5_a_re': s5_a_re, 's5_a_im': s5_a_im, 's5_log_dt': s5_log_dt, 's5_b_re': s5_b_re, 's5_b_im': s5_b_im,
        's5_c_re': s5_c_re, 's5_c_im': s5_c_im, 's5_d': s5_d, 's5_w_glu': s5_w_glu,
        'ssd_conv_w': ssd_conv_w, 'ssd_conv_b': ssd_conv_b, 'ssd_dt_bias': ssd_dt_bias, 'ssd_a_log': ssd_a_log,
        'ssd_d': ssd_d, 'ssd_norm': ssd_norm,
        'hy_conv_w': hy_conv_w, 'hy_conv_b': hy_conv_b, 'hy_w1': hy_w1, 'hy_b1': hy_b1, 'hy_w2': hy_w2,
        'hy_b2': hy_b2, 'hy_w3': hy_w3, 'hy_freq': hy_freq, 'hy_bias': hy_bias,
        'ffn_wg': ffn_wg, 'ffn_wu': ffn_wu, 'ffn_wd': ffn_wd,
        'moe_router': moe_router, 'moe_wg': moe_wg, 'moe_wu': moe_wu, 'moe_wd': moe_wd,
    }
    n_ctx_batch, n_ctx = x_prompt.shape[0], x_prompt.shape[1]
    n_lat = x_sample.shape[1]
    rows = n_lat // GRID_W
    zero_s5 = jnp.zeros((n_ctx_batch, DEPTH, 2, S5_GROUPS, S5_P), jnp.float32)
    zero_ssd = jnp.zeros((n_ctx_batch, DEPTH, 2, SSD_HEADS, SSD_HEADDIM, SSD_DSTATE), jnp.float32)
    y_prompt, new_s5_re, new_s5_im, new_ssd = trunk(x_prompt, c_ctx[None, :], (1, n_ctx),
                                                    zero_s5, zero_s5, zero_ssd, P)
    y_sample, _, _, _ = trunk(x_sample, c, (rows, GRID_W), state_s5_re, state_s5_im, state_ssd, P)
    return (y_prompt, y_sample, new_s5_re, new_s5_im, new_ssd)
```

```python
import functools
import math

import jax
import jax.numpy as jnp
from jax import lax
from jax.experimental import pallas as pl
from jax.experimental.pallas import tpu as pltpu

F32 = jnp.float32
BF16 = jnp.bfloat16
HIGHEST = lax.Precision.HIGHEST

D_MODEL = 4096
DEPTH = 2
GRID_W = 64
EPS = 1e-6
D_S5 = D_MODEL // 4
D_SSD = D_MODEL // 2
D_HY = D_MODEL - D_S5 - D_SSD
S5_H = 16
S5_GROUPS = D_S5 // S5_H
S5_P = 64
S5_Q = 16
SSD_HEADDIM = 64
SSD_HEADS = D_SSD // SSD_HEADDIM
SSD_NGROUPS = 4
SSD_HPG = SSD_HEADS // SSD_NGROUPS
SSD_DSTATE = 128
SSD_CHUNK = 128
SSD_CONV_DIM = D_SSD + 2 * SSD_NGROUPS * SSD_DSTATE
HY_ORDER = 2
HY_BANDS = 16
HY_MIN_DECAY = math.log(1e-2) / 1.5
HY_MAX_DECAY = math.log(1e-2) / 0.3
D_FF = 14336
N_EXPERTS = 8
OFF_Z = D_S5
OFF_XBC = OFF_Z + D_SSD
OFF_DT = OFF_XBC + SSD_CONV_DIM
OFF_HY = OFF_DT + 2 * SSD_HEADS
N_IN = OFF_HY + (HY_ORDER + 1) * D_HY

LANES = 128
VMEM_LIMIT = 56 * 1024 * 1024
LAT_ROWS = 1024
TAIL_TN = 5 * LANES


def _cparams(sem, vmem=VMEM_LIMIT):
    return pltpu.CompilerParams(dimension_semantics=sem, vmem_limit_bytes=vmem)


def _mod_row(i, tm, n_ctx_rows):
    return jnp.maximum((i * tm - n_ctx_rows) // LAT_ROWS + 1, 0)


def _mm_body(*refs, a_silu, has_bias, has_resid):
    a_ref, w_ref = refs[0], refs[1]
    k = 2
    bias_ref = resid_ref = gate_ref = None
    if has_bias:
        bias_ref = refs[k]
        k += 1
    if has_resid:
        resid_ref, gate_ref = refs[k], refs[k + 1]
        k += 2
    o_ref, wbf_ref = refs[k], refs[k + 1]

    @pl.when(pl.program_id(1) == 0)
    def _():
        wbf_ref[...] = w_ref[...].astype(BF16)

    a = a_ref[...]
    if a_silu:
        a = jax.nn.silu(a).astype(BF16)
    acc = jnp.dot(a, wbf_ref[...], preferred_element_type=F32)
    if has_bias:
        acc = acc + bias_ref[...]
    if has_resid:
        acc = resid_ref[...] + gate_ref[...] * acc
    o_ref[...] = acc.astype(o_ref.dtype)


def _matmul(a, w, w_lead, n_cols, *, tm, tn, out_dtype, a_silu=False, bias=None, resid=None, gate=None,
            n_ctx_rows=0, name="matmul"):
    m, kdim = a.shape
    assert m % tm == 0 and n_cols % tn == 0
    grid = (n_cols // tn, m // tm)
    if w.ndim == 3:
        w_spec = pl.BlockSpec((None, kdim, tn), lambda j, i: (w_lead, 0, j))
    else:
        w_spec = pl.BlockSpec((kdim, tn), lambda j, i: (0, j))
    in_specs = [pl.BlockSpec((tm, kdim), lambda j, i: (i, 0)), w_spec]
    args = [a, w]
    if bias is not None:
        in_specs.append(pl.BlockSpec((None, 1, tn), lambda j, i: (w_lead, 0, j)))
        args.append(bias)
    if resid is not None:
        in_specs.append(pl.BlockSpec((tm, tn), lambda j, i: (i, j)))
        in_specs.append(pl.BlockSpec((None, 1, tn), lambda j, i: (_mod_row(i, tm, n_ctx_rows), 0, j)))
        args += [resid, gate]
    return pl.pallas_call(
        functools.partial(_mm_body, a_silu=a_silu, has_bias=bias is not None, has_resid=resid is not None),
        grid=grid,
        in_specs=in_specs,
        out_specs=pl.BlockSpec((tm, tn), lambda j, i: (i, j)),
        out_shape=jax.ShapeDtypeStruct((m, n_cols), out_dtype),
        scratch_shapes=[pltpu.VMEM((kdim, tn), BF16)],
        compiler_params=_cparams(("arbitrary", "arbitrary")),
        name=name,
    )(*args)


def _norm_mod_body(x_ref, w_ref, sc_ref, sh_ref, *rest, with_router):
    x = x_ref[...]
    ms = jnp.mean(x * x, axis=-1, keepdims=True)
    y = x * lax.rsqrt(ms + EPS) * w_ref[...]
    h = y * (1.0 + sc_ref[...]) + sh_ref[...]
    if not with_router:
        (o_ref,) = rest
        o_ref[...] = h.astype(o_ref.dtype)
        return
    r_ref, o_ref, route_ref = rest
    o_ref[...] = h.astype(o_ref.dtype)
    logits = jnp.dot(h, r_ref[...], preferred_element_type=F32, precision=HIGHEST)
    lane = lax.broadcasted_iota(jnp.int32, logits.shape, 1).astype(F32)
    neg = jnp.float32(-jnp.inf)
    lg = jnp.where(lane < N_EXPERTS, logits, neg)
    m1 = jnp.max(lg, axis=-1, keepdims=True)
    i1 = jnp.min(jnp.where(lg == m1, lane, float(LANES)), axis=-1, keepdims=True)
    lg2 = jnp.where(lane == i1, neg, lg)
    m2 = jnp.max(lg2, axis=-1, keepdims=True)
    i2 = jnp.min(jnp.where(lg2 == m2, lane, float(LANES)), axis=-1, keepdims=True)
    e = jnp.exp(m2 - m1)
    g1 = 1.0 / (1.0 + e)
    g2 = e / (1.0 + e)
    route = jnp.where(lane == 0, i1, jnp.where(lane == 1, i2, jnp.where(lane == 2, g1, jnp.where(lane == 3, g2, 0.0))))
    route_ref[...] = route


def _norm_mod(x, w_row, mod3, k_scale, k_shift, n_ctx_rows, router=None, tm=256):
    m, d = x.shape
    in_specs = [
        pl.BlockSpec((tm, d), lambda i: (i, 0)),
        pl.BlockSpec((1, d), lambda i: (0, 0)),
        pl.BlockSpec((None, 1, d), lambda i: (_mod_row(i, tm, n_ctx_rows), 0, k_scale)),
        pl.BlockSpec((None, 1, d), lambda i: (_mod_row(i, tm, n_ctx_rows), 0, k_shift)),
    ]
    args = [x, w_row, mod3, mod3]
    out_specs = pl.BlockSpec((tm, d), lambda i: (i, 0))
    out_shape = jax.ShapeDtypeStruct((m, d), BF16)
    if router is not None:
        in_specs.append(pl.BlockSpec((d, LANES), lambda i: (0, 0)))
        args.append(router)
        out_specs = [out_specs, pl.BlockSpec((tm, LANES), lambda i: (i, 0))]
        out_shape = [out_shape, jax.ShapeDtypeStruct((m, LANES), F32)]
    return pl.pallas_call(
        functools.partial(_norm_mod_body, with_router=router is not None),
        grid=(m // tm,),
        in_specs=in_specs,
        out_specs=out_specs,
        out_shape=out_shape,
        compiler_params=_cparams(("arbitrary",)),
        name="norm_mod",
    )(*args)


def _resid_body(x_ref, f_ref, g_ref, *rest, final):
    y = x_ref[...] + g_ref[...] * f_ref[...]
    if final:
        w_ref, o_ref = rest
        ms = jnp.mean(y * y, axis=-1, keepdims=True)
        y = y * lax.rsqrt(ms + EPS) * w_ref[...]
    else:
        (o_ref,) = rest
    o_ref[...] = y


def _resid(x, f, mod3, k_gate, n_ctx_rows, final_w=None, tm=256):
    m, d = x.shape
    in_specs = [
        pl.BlockSpec((tm, d), lambda i: (i, 0)),
        pl.BlockSpec((tm, d), lambda i: (i, 0)),
        pl.BlockSpec((None, 1, d), lambda i: (_mod_row(i, tm, n_ctx_rows), 0, k_gate)),
    ]
    args = [x, f, mod3]
    if final_w is not None:
        in_specs.append(pl.BlockSpec((1, d), lambda i: (0, 0)))
        args.append(final_w)
    return pl.pallas_call(
        functools.partial(_resid_body, final=final_w is not None),
        grid=(m // tm,),
        in_specs=in_specs,
        out_specs=pl.BlockSpec((tm, d), lambda i: (i, 0)),
        out_shape=jax.ShapeDtypeStruct((m, d), F32),
        compiler_params=_cparams(("arbitrary",)),
        name="resid",
    )(*args)


FFN_ROWS = 1024
FFN_SUB = 512
FFN_TF = 256
FFN_VMEM_LIMIT = 62 * 1024 * 1024


def _ffn_body(ie_ref, ib_ref, nv_ref, x_ref, wg_ref, wu_ref, wd_ref, o_ref):
    it = pl.program_id(0)
    f = pl.program_id(1)
    nv = nv_ref[it]

    @pl.when(jnp.logical_and(f == 0, nv > 0))
    def _():
        o_ref[...] = jnp.zeros_like(o_ref)

    @pl.when(nv > 0)
    def _():
        wg = wg_ref[...].astype(BF16)
        wu = wu_ref[...].astype(BF16)
        wd = wd_ref[...].astype(BF16)

        def sub(s, carry):
            r0 = pl.multiple_of(s * FFN_SUB, FFN_SUB)
            xb = x_ref[pl.ds(r0, FFN_SUB), :]
            g = jnp.dot(xb, wg, preferred_element_type=F32)
            u = jnp.dot(xb, wu, preferred_element_type=F32)
            a = (jax.nn.silu(g) * u).astype(BF16)
            o_ref[pl.ds(r0, FFN_SUB), :] += jnp.dot(a, wd, preferred_element_type=F32)
            return carry

        lax.fori_loop(0, (nv + FFN_SUB - 1) // FFN_SUB, sub, 0)


def _ffn(xs, wg, wu, wd, lead, item_expert, item_block, item_rows):
    r, d = xs.shape
    n_items = item_expert.shape[0]
    dff = wg.shape[-1]
    n_f = dff // FFN_TF

    def f_idx(it, f, nv):
        return jnp.where(nv[it] > 0, f, n_f - 1)

    if wg.ndim == 4:
        wgu_spec = pl.BlockSpec((None, None, d, FFN_TF), lambda it, f, ie, ib, nv: (lead, ie[it], 0, f_idx(it, f, nv)))
        wd_spec = pl.BlockSpec((None, None, FFN_TF, d), lambda it, f, ie, ib, nv: (lead, ie[it], f_idx(it, f, nv), 0))
    else:
        wgu_spec = pl.BlockSpec((None, d, FFN_TF), lambda it, f, ie, ib, nv: (lead, 0, f_idx(it, f, nv)))
        wd_spec = pl.BlockSpec((None, FFN_TF, d), lambda it, f, ie, ib, nv: (lead, f_idx(it, f, nv), 0))
    grid_spec = pltpu.PrefetchScalarGridSpec(
        num_scalar_prefetch=3,
        grid=(n_items, n_f),
        in_specs=[
            pl.BlockSpec((FFN_ROWS, d), lambda it, f, ie, ib, nv: (ib[it], 0), pipeline_mode=pl.Buffered(1)),
            wgu_spec, wgu_spec, wd_spec,
        ],
        out_specs=pl.BlockSpec((FFN_ROWS, d), lambda it, f, ie, ib, nv: (ib[it], 0), pipeline_mode=pl.Buffered(1)),
    )
    return pl.pallas_call(
        _ffn_body,
        grid_spec=grid_spec,
        out_shape=jax.ShapeDtypeStruct((r, d), F32),
        compiler_params=_cparams(("arbitrary", "arbitrary"), vmem=FFN_VMEM_LIMIT),
        name="ffn",
    )(item_expert, item_block, item_rows, xs, wg, wu, wd)


def _moe_dispatch(route, n_tokens):
    e_idx = route[:, :2].astype(jnp.int32)
    gates = route[:, 2:4]
    flat_e = e_idx.reshape(-1)
    n_pairs = flat_e.shape[0]
    order = jnp.argsort(flat_e, stable=True)
    sorted_e = flat_e[order]
    counts = jnp.sum(flat_e[:, None] == jnp.arange(N_EXPERTS)[None, :], axis=0).astype(jnp.int32)
    blocks_per = (counts + FFN_ROWS - 1) // FFN_ROWS
    blk_end = jnp.cumsum(blocks_per)
    blk_start = blk_end - blocks_per
    grp_start = jnp.cumsum(counts) - counts
    rank = jnp.arange(n_pairs, dtype=jnp.int32) - grp_start[sorted_e]
    dest = blk_start[sorted_e] * FFN_ROWS + rank
    n_items = (n_pairs + N_EXPERTS * (FFN_ROWS - 1)) // FFN_ROWS
    n_rows = n_items * FFN_ROWS
    token_of_row = jnp.zeros((n_rows,), jnp.int32).at[dest].set((order // 2).astype(jnp.int32))
    row_of_pair = jnp.zeros((n_pairs,), jnp.int32).at[order].set(dest).reshape(n_tokens, 2)
    blk = jnp.arange(n_items, dtype=jnp.int32)
    n_real = blk_end[-1]
    blk_c = jnp.minimum(blk, n_real - 1)
    item_e = jnp.minimum(jnp.sum(blk_c[:, None] >= blk_end[None, :], axis=1), N_EXPERTS - 1).astype(jnp.int32)
    rows_left = counts[item_e] - (blk_c - blk_start[item_e]) * FFN_ROWS
    item_rows = jnp.where(blk < n_real, jnp.clip(rows_left, 0, FFN_ROWS), 0).astype(jnp.int32)
    return token_of_row, row_of_pair, gates, item_e, blk_c.astype(jnp.int32), item_rows


def _s5_prep(a_re, a_im, log_dt, b_re, b_im, c_re, c_im, d):
    q, g, p, h = S5_Q, S5_GROUPS, S5_P, S5_H
    ein = functools.partial(jnp.einsum, precision=HIGHEST)
    dt = jnp.exp(log_dt)[..., None]
    k = jnp.arange(q + 1, dtype=F32)
    mag = jnp.exp((a_re * dt)[..., None] * k)
    ang = (a_im * dt)[..., None] * k
    pw_re, pw_im = mag * jnp.cos(ang), mag * jnp.sin(ang)
    ab_re, ab_im = pw_re[..., 1], pw_im[..., 1]
    den = a_re * a_re + a_im * a_im
    q_re = ((ab_re - 1.0) * a_re + ab_im * a_im) / den
    q_im = (ab_im * a_re - (ab_re - 1.0) * a_im) / den
    bb_re = q_re[..., None] * b_re[None] - q_im[..., None] * b_im[None]
    bb_im = q_re[..., None] * b_im[None] + q_im[..., None] * b_re[None]
    c_re_t = jnp.swapaxes(c_re, 1, 2)[None, ..., None]
    c_im_t = jnp.swapaxes(c_im, 1, 2)[None, ..., None]
    cp_re = c_re_t * pw_re[:, :, :, None, :] - c_im_t * pw_im[:, :, :, None, :]
    cp_im = c_re_t * pw_im[:, :, :, None, :] + c_im_t * pw_re[:, :, :, None, :]
    kern = ein('dgphk,dgpj->dkghj', cp_re, bb_re) - ein('dgphk,dgpj->dkghj', cp_im, bb_im)
    ii = jnp.arange(q)[:, None]
    jj = jnp.arange(q)[None, :]
    lag = ii - jj
    kf = jnp.where((lag >= 0)[..., None, None, None], kern[0][jnp.clip(lag, 0, q)], 0.0)
    kb = jnp.where((lag <= 0)[..., None, None, None], kern[1][jnp.clip(-lag, 0, q)], 0.0)
    m_intra = jnp.transpose(kf + kb, (2, 1, 4, 0, 3)).reshape(g, q * h, q * h)
    pf_re, pf_im = pw_re[0][..., q - 1 - jnp.arange(q)], pw_im[0][..., q - 1 - jnp.arange(q)]
    pb_re, pb_im = pw_re[1][..., :q], pw_im[1][..., :q]

    def contrib(p_re, p_im, b_r, b_i):
        w_re = p_re[:, :, :, None] * b_r[:, :, None, :] - p_im[:, :, :, None] * b_i[:, :, None, :]
        w_im = p_re[:, :, :, None] * b_i[:, :, None, :] + p_im[:, :, :, None] * b_r[:, :, None, :]
        to_rows = lambda w: jnp.transpose(w, (0, 2, 3, 1)).reshape(g, q * h, p)
        return to_rows(w_re), to_rows(w_im)

    wsf_re, wsf_im = contrib(pf_re, pf_im, bb_re[0], bb_im[0])
    wsb_re, wsb_im = contrib(pb_re, pb_im, bb_re[1], bb_im[1])
    w_state = jnp.concatenate([wsf_re, wsf_im, wsb_re, wsb_im], axis=-1)
    to_cols = lambda w: jnp.transpose(w, (0, 1, 3, 2)).reshape(g, p, q * h)
    fi = jnp.arange(q) + 1
    bi = q - jnp.arange(q)
    w_off = jnp.concatenate([to_cols(cp_re[0][..., fi]), to_cols(-cp_im[0][..., fi]),
                             to_cols(cp_re[1][..., bi]), to_cols(-cp_im[1][..., bi])], axis=1)
    aq_re, aq_im = pw_re[..., q], pw_im[..., q]
    a1 = jnp.concatenate([aq_re[0], aq_re[0], aq_re[1], aq_re[1]], axis=-1)[:, None, :]
    a2 = jnp.concatenate([-aq_im[0], aq_im[0], -aq_im[1], aq_im[1]], axis=-1)[:, None, :]
    d_row = jnp.tile(d.reshape(g, 1, h), (1, 1, q))
    return m_intra.astype(BF16), w_state.astype(BF16), w_off.astype(BF16), a1, a2, d_row


def _s5_body(u_ref, mi_ref, ws_ref, wo_ref, a1_ref, a2_ref, d_ref, x0_ref, y_ref, fin_ref, con_ref, sp_ref,
             *, n_ctx_b, n_ctx_c, n_lat_b, n_lat_c):
    half = 2 * S5_P
    u = u_ref[...]
    ub = u.astype(BF16)
    con_ref[...] = jnp.dot(ub, ws_ref[...], preferred_element_type=F32)

    def scan(row0, nb, nc, s_init, lane0, reverse):
        a1 = jnp.broadcast_to(a1_ref[:, lane0:lane0 + half], (nb, half))
        a2 = jnp.broadcast_to(a2_ref[:, lane0:lane0 + half], (nb, half))
        s = s_init
        order = range(nc - 1, -1, -1) if reverse else range(nc)
        for c in order:
            rows = slice(row0 + c * nb, row0 + (c + 1) * nb)
            sp_ref[rows, lane0:lane0 + half] = s
            s = a1 * s + a2 * pltpu.roll(s, S5_P, 1) + con_ref[rows, lane0:lane0 + half]
        return s

    zeros = jnp.zeros((n_ctx_b, half), F32)
    fin_ref[:, 0:half] = scan(0, n_ctx_b, n_ctx_c, zeros, 0, False)
    fin_ref[:, half:2 * half] = scan(0, n_ctx_b, n_ctx_c, zeros, half, True)
    lat0 = n_ctx_b * n_ctx_c
    scan(lat0, n_lat_b, n_lat_c, x0_ref[:, 0:half], 0, False)
    scan(lat0, n_lat_b, n_lat_c, x0_ref[:, half:2 * half], half, True)
    y = jnp.dot(ub, mi_ref[...], preferred_element_type=F32)
    y = y + jnp.dot(sp_ref[...].astype(BF16), wo_ref[...], preferred_element_type=F32)
    y_ref[...] = y + u * d_ref[...]


def _s5_scan(u_rows, ops, x0, n_ctx_b, n_ctx_c, n_lat_b, n_lat_c):
    g, r, w = u_rows.shape
    m_intra, w_state, w_off, a1, a2, d_row = ops
    mat = pl.BlockSpec((None, w, w), lambda i: (i, 0, 0))
    row = pl.BlockSpec((None, 1, w), lambda i: (i, 0, 0))
    return pl.pallas_call(
        functools.partial(_s5_body, n_ctx_b=n_ctx_b, n_ctx_c=n_ctx_c, n_lat_b=n_lat_b, n_lat_c=n_lat_c),
        grid=(g,),
        in_specs=[pl.BlockSpec((None, r, w), lambda i: (i, 0, 0)), mat, mat, mat, row, row, row,
                  pl.BlockSpec((None, n_lat_b, w), lambda i: (i, 0, 0))],
        out_specs=[pl.BlockSpec((None, r, w), lambda i: (i, 0, 0)),
                   pl.BlockSpec((None, n_ctx_b, w), lambda i: (i, 0, 0))],
        out_shape=[jax.ShapeDtypeStruct((g, r, w), F32), jax.ShapeDtypeStruct((g, n_ctx_b, w), F32)],
        scratch_shapes=[pltpu.VMEM((r, w), F32), pltpu.VMEM((r, w), F32)],
        compiler_params=_cparams(("arbitrary",)),
        name="s5_scan",
    )(u_rows, m_intra, w_state, w_off, a1, a2, d_row, x0)


def _s5_glu_body(y_ref, w_ref, o_ref, wbf_ref):
    @pl.when(pl.program_id(0) == 0)
    def _():
        wbf_ref[...] = w_ref[...].astype(BF16)

    y = jax.nn.gelu(y_ref[...])
    z = jnp.dot(y.astype(BF16), wbf_ref[...], preferred_element_type=F32)
    o_ref[...] = (y * jax.nn.sigmoid(z)).astype(o_ref.dtype)


def _s5_glu(y, w_glu, lead, tm=512):
    m, d = y.shape
    return pl.pallas_call(
        _s5_glu_body,
        grid=(m // tm,),
        in_specs=[pl.BlockSpec((tm, d), lambda i: (i, 0)), pl.BlockSpec((None, d, d), lambda i: (lead, 0, 0))],
        out_specs=pl.BlockSpec((tm, d), lambda i: (i, 0)),
        out_shape=jax.ShapeDtypeStruct((m, d), BF16),
        scratch_shapes=[pltpu.VMEM((d, d), BF16)],
        compiler_params=_cparams(("arbitrary",)),
        name="s5_glu",
    )(y, w_glu)


def _conv3(u, w_ref, b_ref, seg):
    n = u.shape[0]
    assert seg & (seg - 1) == 0
    pos = lax.broadcasted_iota(jnp.int32, u.shape, 0) & (seg - 1)
    prev = jnp.where(pos == 0, 0.0, pltpu.roll(u, 1, 0))
    nxt = jnp.where(pos == seg - 1, 0.0, pltpu.roll(u, n - 1, 0))
    return b_ref[...] + w_ref[0:1, :] * prev + w_ref[1:2, :] * u + w_ref[2:3, :] * nxt


def _softplus(x):
    return jnp.maximum(x, 0.0) + jnp.log(1.0 + jnp.exp(-jnp.abs(x)))


def _ssd_body(*refs, seq, seg, has_init, want_fin):
    (x_ref, b_ref, c_ref, dt_ref, cwx_ref, cwb_ref, cwc_ref, cbx_ref, cbb_ref, cbc_ref,
     dtb_ref, arow_ref, drow_ref) = refs[:13]
    k = 13
    init_ref = None
    if has_init:
        init_ref = refs[k]
        k += 1
    y_ref = refs[k]
    k += 1
    fin_ref = None
    if want_fin:
        fin_ref = refs[k]
        k += 1
    xc_ref, bc_ref, cc_ref, stf_ref, stb_ref = refs[k:k + 5]
    q, p, hpg = SSD_CHUNK, SSD_HEADDIM, SSD_HPG
    wid = hpg * p
    nc = seq // q

    xc_ref[...] = jax.nn.silu(_conv3(x_ref[...], cwx_ref, cbx_ref, seg))
    bc_ref[...] = jax.nn.silu(_conv3(b_ref[...], cwb_ref, cbb_ref, seg))
    cc_ref[...] = jax.nn.silu(_conv3(c_ref[...], cwc_ref, cbc_ref, seg))
    for r in range(hpg):
        cols = slice(r * p, (r + 1) * p)
        if has_init:
            stf_ref[:, cols] = init_ref[0, r]
            stb_ref[:, cols] = init_ref[1, r]
        else:
            stf_ref[:, cols] = jnp.zeros((SSD_DSTATE, p), F32)
            stb_ref[:, cols] = jnp.zeros((SSD_DSTATE, p), F32)

    ii = lax.broadcasted_iota(jnp.int32, (q, q), 0)
    jj = lax.broadcasted_iota(jnp.int32, (q, q), 1)
    tri_lo = (ii >= jj).astype(F32)
    tri_up = (ii <= jj).astype(F32)
    er = lax.broadcasted_iota(jnp.int32, (LANES, wid), 0)
    ec = lax.broadcasted_iota(jnp.int32, (LANES, wid), 1) // p
    exp_f = (er == ec).astype(F32)
    exp_b = (er == ec + hpg).astype(F32)
    neg = jnp.float32(-jnp.inf)
    hdot = functools.partial(jnp.dot, preferred_element_type=F32, precision=HIGHEST)
    bdot = functools.partial(jnp.dot, preferred_element_type=F32)

    def chunk_dt(r0):
        dt = _softplus(dt_ref[pl.ds(r0, q), :] + dtb_ref[...])
        return dt, dt * arow_ref[...]

    def fwd(c, carry):
        r0 = pl.multiple_of(c * q, q)
        dt, adt = chunk_dt(r0)
        cs = hdot(tri_lo, adt)
        rcs = hdot(tri_up, adt)
        cs_t, rcs_t = cs.T, rcs.T
        x = xc_ref[pl.ds(r0, q), :]
        bm = bc_ref[pl.ds(r0, q), :]
        cm = cc_ref[pl.ds(r0, q), :].astype(BF16)
        cb = lax.dot_general(cm, bm.astype(BF16), (((1,), (1,)), ((), ())), preferred_element_type=F32)
        xs_f = x * hdot(dt, exp_f)
        xs_b = x * hdot(dt, exp_b)
        csx = hdot(cs, exp_f)
        parts = []
        for r in range(hpg):
            cols = slice(r * p, (r + 1) * p)
            lf = jnp.exp(jnp.where(ii >= jj, cs[:, r:r + 1] - cs_t[r:r + 1, :], neg))
            lb = jnp.exp(jnp.where(jj >= ii, rcs[:, hpg + r:hpg + r + 1] - rcs_t[hpg + r:hpg + r + 1, :], neg))
            lhs = jnp.concatenate([cb * lf, cb * lb], axis=1).astype(BF16)
            rhs = jnp.concatenate([xs_f[:, cols], xs_b[:, cols]], axis=0).astype(BF16)
            parts.append(bdot(lhs, rhs))
        y = jnp.concatenate(parts, axis=1)
        st = stf_ref[...]
        y = y + bdot(cm, st.astype(BF16)) * jnp.exp(csx) + drow_ref[...] * x
        y_ref[pl.ds(r0, q), :] = y
        tot = csx[q - 1:q, :]
        dec = jnp.exp(tot - csx)
        stf_ref[...] = st * jnp.exp(tot) + bdot(bm.T.astype(BF16), (xs_f * dec).astype(BF16))
        return carry

    def bwd(k_it, carry):
        c = nc - 1 - k_it
        r0 = pl.multiple_of(c * q, q)
        dt, adt = chunk_dt(r0)
        rcsx = hdot(hdot(tri_up, adt), exp_b)
        x = xc_ref[pl.ds(r0, q), :]
        bm = bc_ref[pl.ds(r0, q), :]
        cm = cc_ref[pl.ds(r0, q), :].astype(BF16)
        xs_b = x * hdot(dt, exp_b)
        st = stb_ref[...]
        y_ref[pl.ds(r0, q), :] += bdot(cm, st.astype(BF16)) * jnp.exp(rcsx)
        tot = rcsx[0:1, :]
        dec = jnp.exp(tot - rcsx)
        stb_ref[...] = st * jnp.exp(tot) + bdot(bm.T.astype(BF16), (xs_b * dec).astype(BF16))
        return carry

    lax.fori_loop(0, nc, fwd, 0)
    lax.fori_loop(0, nc, bwd, 0)
    if want_fin:
        for r in range(hpg):
            cols = slice(r * p, (r + 1) * p)
            fin_ref[0, r] = stf_ref[:, cols]
            fin_ref[1, r] = stb_ref[:, cols]


def _ssd_scan(proj, dtg, conv_w, conv_b, dtb_rows, a_rows, d_rows, init, *, row0, n_b, seq, seg, want_fin):
    q, p, hpg, ng, ns = SSD_CHUNK, SSD_HEADDIM, SSD_HPG, SSD_NGROUPS, SSD_DSTATE
    wid = hpg * p
    rb0 = row0 // seq
    xcol0 = OFF_XBC // wid
    bcol0 = (OFF_XBC + D_SSD) // ns
    ccol0 = bcol0 + ng
    in_specs = [
        pl.BlockSpec((seq, wid), lambda b, g: (rb0 + b, xcol0 + g)),
        pl.BlockSpec((seq, ns), lambda b, g: (rb0 + b, bcol0 + g)),
        pl.BlockSpec((seq, ns), lambda b, g: (rb0 + b, ccol0 + g)),
        pl.BlockSpec((None, seq, LANES), lambda b, g: (g, rb0 + b, 0)),
        pl.BlockSpec((3, wid), lambda b, g: (0, g)),
        pl.BlockSpec((3, ns), lambda b, g: (0, D_SSD // ns + g)),
        pl.BlockSpec((3, ns), lambda b, g: (0, D_SSD // ns + ng + g)),
        pl.BlockSpec((1, wid), lambda b, g: (0, g)),
        pl.BlockSpec((1, ns), lambda b, g: (0, D_SSD // ns + g)),
        pl.BlockSpec((1, ns), lambda b, g: (0, D_SSD // ns + ng + g)),
        pl.BlockSpec((None, 1, LANES), lambda b, g: (g, 0, 0)),
        pl.BlockSpec((None, 1, LANES), lambda b, g: (g, 0, 0)),
        pl.BlockSpec((None, 1, wid), lambda b, g: (g, 0, 0)),
    ]
    args = [proj, proj, proj, dtg, conv_w, conv_w, conv_w, conv_b, conv_b, conv_b, dtb_rows, a_rows, d_rows]
    st_spec = pl.BlockSpec((None, 2, hpg, ns, p), lambda b, g: (b, 0, g, 0, 0))
    if init is not None:
        in_specs.append(st_spec)
        args.append(init)
    out_specs = [pl.BlockSpec((seq, wid), lambda b, g: (b, g))]
    out_shape = [jax.ShapeDtypeStruct((n_b * seq, D_SSD), F32)]
    if want_fin:
        out_specs.append(st_spec)
        out_shape.append(jax.ShapeDtypeStruct((n_b, 2, SSD_HEADS, ns, p), F32))
    res = pl.pallas_call(
        functools.partial(_ssd_body, seq=seq, seg=seg, has_init=init is not None, want_fin=want_fin),
        grid=(n_b, ng),
        in_specs=in_specs,
        out_specs=out_specs,
        out_shape=out_shape,
        scratch_shapes=[pltpu.VMEM((seq, wid), F32), pltpu.VMEM((seq, ns), F32), pltpu.VMEM((seq, ns), F32),
                        pltpu.VMEM((ns, wid), F32), pltpu.VMEM((ns, wid), F32)],
        compiler_params=_cparams(("arbitrary", "arbitrary")),
        name="ssd_scan",
    )(*args)
    return res if want_fin else (res[0], None)


def _ssd_norm_body(y_ref, z0_ref, z1_ref, w_ref, o_ref):
    z = jnp.concatenate([z0_ref[...], z1_ref[...]], axis=1)
    y = y_ref[...] * jax.nn.silu(z)
    ms = jnp.mean(y * y, axis=-1, keepdims=True)
    o_ref[...] = (y * lax.rsqrt(ms + EPS) * w_ref[...]).astype(o_ref.dtype)


def _ssd_norm(y, proj, norm_w, tm=256):
    m, d = y.shape
    half = d // 2
    zb = OFF_Z // half
    return pl.pallas_call(
        _ssd_norm_body,
        grid=(m // tm,),
        in_specs=[pl.BlockSpec((tm, d), lambda i: (i, 0)),
                  pl.BlockSpec((tm, half), lambda i: (i, zb)),
                  pl.BlockSpec((tm, half), lambda i: (i, zb + 1)),
                  pl.BlockSpec((1, d), lambda i: (0, 0))],
        out_specs=pl.BlockSpec((tm, d), lambda i: (i, 0)),
        out_shape=jax.ShapeDtypeStruct((m, d), BF16),
        compiler_params=_cparams(("arbitrary",)),
        name="ssd_norm",
    )(y, proj, proj, norm_w)


def _hyena_filters(seq, w1, b1, w2, b2, w3, freq):
    t = jnp.arange(seq, dtype=F32) / seq
    bands = jnp.linspace(1e-4, HY_BANDS - 1, HY_BANDS, dtype=F32)
    ang = 2.0 * math.pi * t[:, None] * bands[None, :]
    feat = jnp.concatenate([t[:, None], jnp.cos(ang), -jnp.sin(ang)], axis=-1)
    dot = functools.partial(jnp.dot, precision=HIGHEST)
    hdn = jnp.sin(freq[0] * (dot(feat, w1) + b1))
    hdn = jnp.sin(freq[1] * (dot(hdn, w2) + b2))
    filt = dot(hdn, w3).reshape(seq, 2, HY_ORDER, D_HY)
    deltas = jnp.abs(jnp.linspace(HY_MIN_DECAY, HY_MAX_DECAY, D_HY, dtype=F32))
    filt = filt * jnp.exp(-t[:, None] * deltas[None, :])[:, None, None, :]
    hf, hb = filt[:, 0], filt[:, 1]
    return jnp.concatenate([hf[:1] + hb[:1], hf[1:], jnp.zeros_like(hf[:1]), jnp.flip(hb[1:], axis=0)], axis=0)


def _split_bf16(x):
    hi = x.astype(BF16)
    return hi, (x - hi.astype(F32)).astype(BF16)


def _dft_ops(seq):
    n = 2 * seq
    f = jnp.arange(seq, dtype=jnp.int32)[:, None]
    t = jnp.arange(seq, dtype=jnp.int32)[None, :]
    ang = ((f * t) % n).astype(F32) * (math.pi / seq)
    cos_ft, sin_ft = jnp.cos(ang), jnp.sin(ang)
    nyq = jnp.where(t % 2 == 0, 1.0, -1.0).astype(F32)
    fwd = jnp.concatenate([cos_ft, nyq, -sin_ft[1:]], axis=0)
    wgt = jnp.where(f == 0, 1.0, 2.0).astype(F32) / n
    inv = jnp.concatenate([(wgt * cos_ft).T, nyq.T / n, (-(2.0 / n) * sin_ft[1:]).T], axis=1)
    return _split_bf16(fwd) + _split_bf16(inv)


def _hyena_spectra(kfilt, seq):
    kf = jnp.fft.rfft(kfilt, n=2 * seq, axis=0)
    kr = jnp.real(kf).astype(F32)
    ki = jnp.imag(kf).astype(F32)
    kr_main = jnp.transpose(kr[:seq], (1, 0, 2))
    ki_main = jnp.transpose(ki[:seq], (1, 0, 2)).at[:, 0].set(0.0)
    kr_nyq = kr_main.at[:, 0].set(kr[seq])
    return kr_main, ki_main, kr_nyq


def _dot3(a_hi, a_lo, b_hi, b_lo):
    d = functools.partial(jnp.dot, preferred_element_type=F32)
    return d(a_hi, b_hi) + (d(a_hi, b_lo) + d(a_lo, b_hi))


def _hyena_body(v_ref, x1_ref, x2_ref, cwv_ref, cw1_ref, cw2_ref, cbv_ref, cb1_ref, cb2_ref,
                kr_ref, ki_ref, kn_ref, bias_ref, fh_ref, fl_ref, gh_ref, gl_ref, o_ref, *, seq, seg):
    z = _conv3(v_ref[...], cwv_ref, cbv_ref, seg)
    gates = (_conv3(x1_ref[...], cw1_ref, cb1_ref, seg), _conv3(x2_ref[...], cw2_ref, cb2_ref, seg))
    for o in range(HY_ORDER):
        z_hi, z_lo = _split_bf16(z)
        zf = _dot3(fh_ref[...], fl_ref[...], z_hi, z_lo)
        a, b = zf[:seq], zf[seq:]
        pre = a * kr_ref[o] - b * ki_ref[o]
        pim = a * ki_ref[o] + b * kn_ref[o]
        p_hi, p_lo = _split_bf16(jnp.concatenate([pre, pim], axis=0))
        conv = _dot3(gh_ref[...], gl_ref[...], p_hi, p_lo)
        z = gates[o] * (conv + bias_ref[o:o + 1, :] * z)
    o_ref[...] = z.astype(o_ref.dtype)


def _hyena(proj_hy, conv_w, conv_b, spectra, bias, dft, *, row0, n_b, seq, seg, tc):
    kr, ki, kn = spectra
    fh, fl, gh, gl = dft
    rb0 = row0 // seq
    ncb = D_HY // tc
    u_spec = lambda part: pl.BlockSpec((seq, tc), lambda cb, b: (rb0 + b, part * ncb + cb))
    w_spec = lambda part: pl.BlockSpec((3, tc), lambda cb, b: (0, part * ncb + cb))
    b_spec = lambda part: pl.BlockSpec((1, tc), lambda cb, b: (0, part * ncb + cb))
    k_spec = pl.BlockSpec((HY_ORDER, seq, tc), lambda cb, b: (0, 0, cb))
    const = lambda shape: pl.BlockSpec(shape, lambda cb, b: (0, 0), pipeline_mode=pl.Buffered(1))
    return pl.pallas_call(
        functools.partial(_hyena_body, seq=seq, seg=seg),
        grid=(ncb, n_b),
        in_specs=[u_spec(0), u_spec(1), u_spec(2), w_spec(0), w_spec(1), w_spec(2), b_spec(0), b_spec(1), b_spec(2),
                  k_spec, k_spec, k_spec, pl.BlockSpec((HY_ORDER, tc), lambda cb, b: (0, cb)),
                  const((2 * seq, seq)), const((2 * seq, seq)), const((seq, 2 * seq)), const((seq, 2 * seq))],
        out_specs=pl.BlockSpec((seq, tc), lambda cb, b: (b, cb)),
        out_shape=jax.ShapeDtypeStruct((n_b * seq, D_HY), BF16),
        compiler_params=_cparams(("arbitrary", "arbitrary")),
        name="hyena",
    )(proj_hy, proj_hy, proj_hy, conv_w, conv_w, conv_w, conv_b, conv_b, conv_b, kr, ki, kn, bias, fh, fl, gh, gl)


def _s5_mixer(proj, p, l, dims):
    n_cb, n_cs, n_lb, n_ls = dims
    t_ctx = n_cb * n_cs
    nq_c, nq_l = n_cs // S5_Q, n_ls // S5_Q
    u = proj[:, :D_S5]

    def to_rows(a, nb, nq):
        a = jnp.transpose(a.reshape(nb, nq, S5_Q, S5_GROUPS, S5_H), (3, 1, 0, 2, 4))
        return a.reshape(S5_GROUPS, nq * nb, S5_Q * S5_H)

    def from_rows(a, nb, nq):
        a = jnp.transpose(a.reshape(S5_GROUPS, nq, nb, S5_Q, S5_H), (2, 1, 3, 0, 4))
        return a.reshape(nb * nq * S5_Q, D_S5)

    u_rows = jnp.concatenate([to_rows(u[:t_ctx], n_cb, nq_c), to_rows(u[t_ctx:], n_lb, nq_l)], axis=1)
    ops = _s5_prep(p['s5_a_re'][l], p['s5_a_im'][l], p['s5_log_dt'][l], p['s5_b_re'][l], p['s5_b_im'][l],
                   p['s5_c_re'][l], p['s5_c_im'][l], p['s5_d'][l])
    sre, sim = p['state_s5_re'][:, l], p['state_s5_im'][:, l]
    x0 = jnp.transpose(jnp.concatenate([sre[:, 0], sim[:, 0], sre[:, 1], sim[:, 1]], axis=-1), (1, 0, 2))
    y_rows, fin = _s5_scan(u_rows, ops, x0, n_cb, nq_c, n_lb, nq_l)
    y = jnp.concatenate([from_rows(y_rows[:, :nq_c * n_cb], n_cb, nq_c),
                         from_rows(y_rows[:, nq_c * n_cb:], n_lb, nq_l)], axis=0)
    y = _s5_glu(y, p['s5_w_glu'], l)
    fin = jnp.transpose(fin.reshape(S5_GROUPS, n_cb, 4, S5_P), (1, 2, 0, 3))
    return y, fin[:, 0::2], fin[:, 1::2]


def _ssd_mixer(proj, dt_raw, p, l, dims):
    n_cb, n_cs, n_lb, n_ls = dims
    t_ctx = n_cb * n_cs
    t_all = proj.shape[0]
    pad_heads = jnp.zeros((SSD_NGROUPS, LANES - 2 * SSD_HPG), F32)

    def per_group(a):
        return jnp.concatenate([a[0].reshape(SSD_NGROUPS, SSD_HPG), a[1].reshape(SSD_NGROUPS, SSD_HPG),
                                pad_heads], axis=1)[:, None, :]

    dtb_rows = per_group(p['ssd_dt_bias'][l])
    a_rows = per_group(-jnp.exp(p['ssd_a_log'][l]))
    d_rows = jnp.repeat(p['ssd_d'][l], SSD_HEADDIM).reshape(SSD_NGROUPS, 1, SSD_HPG * SSD_HEADDIM)
    dtr = dt_raw.reshape(t_all, 2, SSD_NGROUPS, SSD_HPG)
    dtg = jnp.concatenate([jnp.transpose(dtr, (2, 0, 1, 3)).reshape(SSD_NGROUPS, t_all, 2 * SSD_HPG),
                           jnp.zeros((SSD_NGROUPS, t_all, LANES - 2 * SSD_HPG), F32)], axis=-1)
    cw, cb = p['ssd_conv_w'][l], p['ssd_conv_b'][l][None]
    y_c, fs = _ssd_scan(proj, dtg, cw, cb, dtb_rows, a_rows, d_rows, None,
                        row0=0, n_b=n_cb, seq=n_cs, seg=n_cs, want_fin=True)
    y_l, _ = _ssd_scan(proj, dtg, cw, cb, dtb_rows, a_rows, d_rows, jnp.swapaxes(p['state_ssd'][:, l], -1, -2),
                       row0=t_ctx, n_b=n_lb, seq=n_ls, seg=GRID_W, want_fin=False)
    y = _ssd_norm(jnp.concatenate([y_c, y_l], axis=0), proj, p['ssd_norm'][l][None])
    return y, jnp.swapaxes(fs, -1, -2)


def _hyena_mixer(proj_t, p, l, dims):
    n_cb, n_cs, n_lb, n_ls = dims
    hcw, hcb = p['hy_conv_w'][l], p['hy_conv_b'][l][None]
    out = []
    for (row0, nb, seq, seg, tc) in ((0, n_cb, n_cs, n_cs, 512), (n_cb * n_cs, n_lb, n_ls, GRID_W, 256)):
        kfilt = _hyena_filters(seq, p['hy_w1'][l], p['hy_b1'][l], p['hy_w2'][l], p['hy_b2'][l], p['hy_w3'][l],
                               p['hy_freq'][l])
        out.append(_hyena(proj_t, hcw, hcb, _hyena_spectra(kfilt, seq), p['hy_bias'][l], _dft_ops(seq),
                          row0=row0, n_b=nb, seq=seq, seg=seg, tc=tc))
    return jnp.concatenate(out, axis=0)


def _in_proj(h, w_in, l):
    d = h.shape[1]
    proj = _matmul(h, w_in, l, OFF_DT, tm=1024, tn=512, out_dtype=F32, name="w_in_main")
    n_used = N_IN - OFF_DT
    n_tail = -(-n_used // TAIL_TN) * TAIL_TN
    w_tail = jnp.concatenate([w_in[l][:, OFF_HY:], w_in[l][:, OFF_DT:OFF_HY],
                              jnp.zeros((d, n_tail - n_used), F32)], axis=1)
    proj_t = _matmul(h, w_tail, 0, n_tail, tm=1024, tn=TAIL_TN, out_dtype=F32, name="w_in_tail")
    dt_raw = proj_t[:, N_IN - OFF_HY:n_used]
    return proj, proj_t, dt_raw


def _moe(h, route, wg, wu, wd, lead):
    t_all = h.shape[0]
    token_of_row, row_of_pair, gates, item_e, item_b, item_rows = _moe_dispatch(route, t_all)
    ys = _ffn(jnp.take(h, token_of_row, axis=0), wg, wu, wd, lead, item_e, item_b, item_rows)
    return (gates[:, 0:1] * jnp.take(ys, row_of_pair[:, 0], axis=0)
            + gates[:, 1:2] * jnp.take(ys, row_of_pair[:, 1], axis=0))


def kernel(x_prompt, x_sample, state_s5_re, state_s5_im, state_ssd, c, c_ctx, ada_w, ada_b, norm1, norm2, final_norm, w_in, w_out, s5_a_re, s5_a_im, s5_log_dt, s5_b_re, s5_b_im, s5_c_re, s5_c_im, s5_d, s5_w_glu, ssd_conv_w, ssd_conv_b, ssd_dt_bias, ssd_a_log, ssd_d, ssd_norm, hy_conv_w, hy_conv_b, hy_w1, hy_b1, hy_w2, hy_b2, hy_w3, hy_freq, hy_bias, ffn_wg, ffn_wu, ffn_wd, moe_router, moe_wg, moe_wu, moe_wd):
    n_cb, n_cs, d = x_prompt.shape
    n_lb, n_ls, _ = x_sample.shape
    assert d == D_MODEL and n_ls == LAT_ROWS and n_cs % SSD_CHUNK == 0 and n_ls % SSD_CHUNK == 0
    t_ctx, t_lat = n_cb * n_cs, n_lb * n_ls
    t_all = t_ctx + t_lat
    assert t_ctx % LAT_ROWS == 0 and t_all % FFN_ROWS == 0 and 1 + n_lb <= 3
    x = jnp.concatenate([x_prompt.reshape(t_ctx, d), x_sample.reshape(t_lat, d)], axis=0)

    cond = jnp.zeros((8, d), F32).at[0].set(c_ctx).at[1:1 + n_lb].set(c)
    mod = jnp.concatenate(
        [_matmul(cond, ada_w, l, 6 * d, tm=8, tn=512, out_dtype=F32, a_silu=True,
                 bias=ada_b.reshape(DEPTH, 1, 6 * d), name="ada_mod")[None] for l in range(DEPTH)], axis=0)
    mod3 = mod[:, :3].reshape(DEPTH, 3, 1, 6 * d)

    p = dict(state_s5_re=state_s5_re, state_s5_im=state_s5_im, state_ssd=state_ssd,
             s5_a_re=s5_a_re, s5_a_im=s5_a_im, s5_log_dt=s5_log_dt, s5_b_re=s5_b_re, s5_b_im=s5_b_im,
             s5_c_re=s5_c_re, s5_c_im=s5_c_im, s5_d=s5_d, s5_w_glu=s5_w_glu,
             ssd_conv_w=ssd_conv_w, ssd_conv_b=ssd_conv_b, ssd_dt_bias=ssd_dt_bias, ssd_a_log=ssd_a_log,
             ssd_d=ssd_d, ssd_norm=ssd_norm, hy_conv_w=hy_conv_w, hy_conv_b=hy_conv_b, hy_w1=hy_w1, hy_b1=hy_b1,
             hy_w2=hy_w2, hy_b2=hy_b2, hy_w3=hy_w3, hy_freq=hy_freq, hy_bias=hy_bias)
    dims = (n_cb, n_cs, n_lb, n_ls)
    fin_re, fin_im, fin_ssd = [], [], []
    for l in range(DEPTH):
        m3 = mod3[l]
        h = _norm_mod(x, norm1[l][None], m3, 1, 0, t_ctx)
        proj, proj_t, dt_raw = _in_proj(h, w_in, l)
        y_s5, f_re, f_im = _s5_mixer(proj, p, l, dims)
        y_ssd, f_ssd = _ssd_mixer(proj, dt_raw, p, l, dims)
        y_hy = _hyena_mixer(proj_t, p, l, dims)
        fin_re.append(f_re)
        fin_im.append(f_im)
        fin_ssd.append(f_ssd)
        y_mix = jnp.concatenate([y_s5, y_ssd, y_hy], axis=1)
        x = _matmul(y_mix, w_out, l, d, tm=1024, tn=512, out_dtype=F32, resid=x, gate=m3[:, :, 2 * d:3 * d],
                    n_ctx_rows=t_ctx, name="w_out")
        if l % 2 == 0:
            h = _norm_mod(x, norm2[l][None], m3, 4, 3, t_ctx)
            n_items = t_all // FFN_ROWS
            f = _ffn(h, ffn_wg, ffn_wu, ffn_wd, l // 2, jnp.zeros((n_items,), jnp.int32),
                     jnp.arange(n_items, dtype=jnp.int32), jnp.full((n_items,), FFN_ROWS, jnp.int32))
        else:
            router = jnp.concatenate([moe_router[l // 2], jnp.zeros((d, LANES - N_EXPERTS), F32)], axis=1)
            h, route = _norm_mod(x, norm2[l][None], m3, 4, 3, t_ctx, router=router)
            f = _moe(h, route, moe_wg, moe_wu, moe_wd, l // 2)
        x = _resid(x, f, m3, 5, t_ctx, final_w=final_norm[None] if l == DEPTH - 1 else None)

    y_prompt = x[:t_ctx].reshape(n_cb, n_cs, d)
    y_sample = x[t_ctx:].reshape(n_lb, n_ls, d)
    return (y_prompt, y_sample, jnp.stack(fin_re, axis=1), jnp.stack(fin_im, axis=1), jnp.stack(fin_ssd, axis=1))
```

```python
import functools
import math

import jax
import jax.numpy as jnp
from jax import lax
from jax.experimental import pallas as pl
from jax.experimental.pallas import tpu as pltpu

F32 = jnp.float32
BF16 = jnp.bfloat16
HIGHEST = lax.Precision.HIGHEST

D_MODEL = 4096
DEPTH = 2
GRID_W = 64
EPS = 1e-6
D_S5 = D_MODEL // 4
D_SSD = D_MODEL // 2
D_HY = D_MODEL - D_S5 - D_SSD
S5_H = 16
S5_GROUPS = D_S5 // S5_H
S5_P = 64
S5_Q = 16
SSD_HEADDIM = 64
SSD_HEADS = D_SSD // SSD_HEADDIM
SSD_NGROUPS = 4
SSD_HPG = SSD_HEADS // SSD_NGROUPS
SSD_DSTATE = 128
SSD_CHUNK = 128
SSD_CONV_DIM = D_SSD + 2 * SSD_NGROUPS * SSD_DSTATE
HY_ORDER = 2
HY_BANDS = 16
HY_MIN_DECAY = math.log(1e-2) / 1.5
HY_MAX_DECAY = math.log(1e-2) / 0.3
D_FF = 14336
N_EXPERTS = 8
OFF_Z = D_S5
OFF_XBC = OFF_Z + D_SSD
OFF_DT = OFF_XBC + SSD_CONV_DIM
OFF_HY = OFF_DT + 2 * SSD_HEADS
N_IN = OFF_HY + (HY_ORDER + 1) * D_HY

LANES = 128
VMEM_LIMIT = 56 * 1024 * 1024
LAT_ROWS = 1024
PROJ_TN = 512
N_PROJ = -(-N_IN // PROJ_TN) * PROJ_TN
PROJ_HY = OFF_DT
PROJ_DT = PROJ_HY + (HY_ORDER + 1) * D_HY


def _cparams(sem, vmem=VMEM_LIMIT):
    return pltpu.CompilerParams(dimension_semantics=sem, vmem_limit_bytes=vmem)


def _mod_row(i, tm, n_ctx_rows):
    return jnp.maximum((i * tm - n_ctx_rows) // LAT_ROWS + 1, 0)


def _mm_body(*refs, n_a, a_silu, has_bias, has_resid):
    a_refs, w_ref = refs[:n_a], refs[n_a]
    k = n_a + 1
    bias_ref = resid_ref = gate_ref = None
    if has_bias:
        bias_ref = refs[k]
        k += 1
    if has_resid:
        resid_ref, gate_ref = refs[k], refs[k + 1]
        k += 2
    o_ref, wbf_ref = refs[k], refs[k + 1]

    @pl.when(pl.program_id(1) == 0)
    def _():
        wbf_ref[...] = w_ref[...].astype(BF16)

    acc = None
    k0 = 0
    for a_ref in a_refs:
        a = a_ref[...]
        if a_silu:
            a = jax.nn.silu(a).astype(BF16)
        part = jnp.dot(a, wbf_ref[k0:k0 + a.shape[1], :], preferred_element_type=F32)
        acc = part if acc is None else acc + part
        k0 += a.shape[1]
    if has_bias:
        acc = acc + bias_ref[...]
    if has_resid:
        acc = resid_ref[...] + gate_ref[...] * acc
    o_ref[...] = acc.astype(o_ref.dtype)


def _matmul(a, w, w_lead, n_cols, *, tm, tn, out_dtype, a_silu=False, bias=None, resid=None, gate=None,
            n_ctx_rows=0, name="matmul"):
    a_list = list(a) if isinstance(a, (list, tuple)) else [a]
    m = a_list[0].shape[0]
    kdim = sum(x.shape[1] for x in a_list)
    assert m % tm == 0 and n_cols % tn == 0 and kdim == w.shape[-2]
    grid = (n_cols // tn, m // tm)
    w_spec = pl.BlockSpec((None, kdim, tn), lambda j, i: (w_lead, 0, j))
    in_specs = [pl.BlockSpec((tm, x.shape[1]), lambda j, i: (i, 0)) for x in a_list] + [w_spec]
    args = a_list + [w]
    if bias is not None:
        in_specs.append(pl.BlockSpec((None, 1, tn), lambda j, i: (w_lead, 0, j)))
        args.append(bias)
    if resid is not None:
        in_specs.append(pl.BlockSpec((tm, tn), lambda j, i: (i, j)))
        in_specs.append(pl.BlockSpec((None, 1, tn), lambda j, i: (_mod_row(i, tm, n_ctx_rows), 0, j)))
        args += [resid, gate]
    return pl.pallas_call(
        functools.partial(_mm_body, n_a=len(a_list), a_silu=a_silu, has_bias=bias is not None,
                          has_resid=resid is not None),
        grid=grid,
        in_specs=in_specs,
        out_specs=pl.BlockSpec((tm, tn), lambda j, i: (i, j)),
        out_shape=jax.ShapeDtypeStruct((m, n_cols), out_dtype),
        scratch_shapes=[pltpu.VMEM((kdim, tn), BF16)],
        compiler_params=_cparams(("arbitrary", "arbitrary")),
        name=name,
    )(*args)


def _norm_mod_body(x_ref, w_ref, sc_ref, sh_ref, *rest, with_router):
    x = x_ref[...]
    ms = jnp.mean(x * x, axis=-1, keepdims=True)
    y = x * lax.rsqrt(ms + EPS) * w_ref[...]
    h = y * (1.0 + sc_ref[...]) + sh_ref[...]
    if not with_router:
        (o_ref,) = rest
        o_ref[...] = h.astype(o_ref.dtype)
        return
    r_ref, o_ref, route_ref = rest
    half = h.shape[1] // 2
    hb = h.astype(BF16)
    lo = lax.bitcast_convert_type(hb[:, :half].astype(F32), jnp.uint32)
    hi = lax.bitcast_convert_type(hb[:, half:].astype(F32), jnp.uint32)
    o_ref[...] = lax.bitcast_convert_type((lo >> 16) | hi, F32)
    logits = jnp.dot(h, r_ref[...], preferred_element_type=F32, precision=HIGHEST)
    lane = lax.broadcasted_iota(jnp.int32, logits.shape, 1).astype(F32)
    neg = jnp.float32(-jnp.inf)
    lg = jnp.where(lane < N_EXPERTS, logits, neg)
    m1 = jnp.max(lg, axis=-1, keepdims=True)
    i1 = jnp.min(jnp.where(lg == m1, lane, float(LANES)), axis=-1, keepdims=True)
    lg2 = jnp.where(lane == i1, neg, lg)
    m2 = jnp.max(lg2, axis=-1, keepdims=True)
    i2 = jnp.min(jnp.where(lg2 == m2, lane, float(LANES)), axis=-1, keepdims=True)
    e = jnp.exp(m2 - m1)
    g1 = 1.0 / (1.0 + e)
    g2 = e / (1.0 + e)
    route = jnp.where(lane == 0, i1, jnp.where(lane == 1, i2, jnp.where(lane == 2, g1, jnp.where(lane == 3, g2, 0.0))))
    route_ref[...] = route


def _norm_mod(x, w_row, mod3, k_scale, k_shift, n_ctx_rows, router=None, tm=256):
    m, d = x.shape
    in_specs = [
        pl.BlockSpec((tm, d), lambda i: (i, 0)),
        pl.BlockSpec((1, d), lambda i: (0, 0)),
        pl.BlockSpec((None, 1, d), lambda i: (_mod_row(i, tm, n_ctx_rows), 0, k_scale)),
        pl.BlockSpec((None, 1, d), lambda i: (_mod_row(i, tm, n_ctx_rows), 0, k_shift)),
    ]
    args = [x, w_row, mod3, mod3]
    out_specs = pl.BlockSpec((tm, d), lambda i: (i, 0))
    out_shape = jax.ShapeDtypeStruct((m, d), BF16)
    if router is not None:
        in_specs.append(pl.BlockSpec((d, LANES), lambda i: (0, 0)))
        args.append(router)
        out_specs = [pl.BlockSpec((tm, d // 2), lambda i: (i, 0)), pl.BlockSpec((tm, LANES), lambda i: (i, 0))]
        out_shape = [jax.ShapeDtypeStruct((m, d // 2), F32), jax.ShapeDtypeStruct((m, LANES), F32)]
    return pl.pallas_call(
        functools.partial(_norm_mod_body, with_router=router is not None),
        grid=(m // tm,),
        in_specs=in_specs,
        out_specs=out_specs,
        out_shape=out_shape,
        compiler_params=_cparams(("arbitrary",)),
        name="norm_mod",
    )(*args)


def _resid_body(*refs, n_f, final):
    x_ref, f_refs, g_ref = refs[0], refs[1:1 + n_f], refs[1 + n_f]
    rest = refs[2 + n_f:]
    if n_f == 1:
        f = f_refs[0][...]
    else:
        route_ref, rest = rest[0], rest[1:]
        f = route_ref[:, 2:3] * f_refs[0][...] + route_ref[:, 3:4] * f_refs[1][...]
    y = x_ref[...] + g_ref[...] * f
    if final:
        w_ref, o_ref = rest
        ms = jnp.mean(y * y, axis=-1, keepdims=True)
        y = y * lax.rsqrt(ms + EPS) * w_ref[...]
    else:
        (o_ref,) = rest
    o_ref[...] = y


def _resid(x, fs, mod3, k_gate, n_ctx_rows, route=None, final_w=None, tm=256):
    m, d = x.shape
    tok = pl.BlockSpec((tm, d), lambda i: (i, 0))
    in_specs = [tok] + [tok] * len(fs) + [
        pl.BlockSpec((None, 1, d), lambda i: (_mod_row(i, tm, n_ctx_rows), 0, k_gate))]
    args = [x, *fs, mod3]
    if route is not None:
        in_specs.append(pl.BlockSpec((tm, LANES), lambda i: (i, 0)))
        args.append(route)
    if final_w is not None:
        in_specs.append(pl.BlockSpec((1, d), lambda i: (0, 0)))
        args.append(final_w)
    return pl.pallas_call(
        functools.partial(_resid_body, n_f=len(fs), final=final_w is not None),
        grid=(m // tm,),
        in_specs=in_specs,
        out_specs=pl.BlockSpec((tm, d), lambda i: (i, 0)),
        out_shape=jax.ShapeDtypeStruct((m, d), F32),
        compiler_params=_cparams(("arbitrary",)),
        name="resid",
    )(*args)


FFN_ROWS = 1024
FFN_TF = 256
FFN_KC = 1024
FFN_NC = 512
FFN_VMEM_LIMIT = 62 * 1024 * 1024
FFN_DENSE_ROWS = (FFN_ROWS,)
FFN_MOE_ROWS = (256, 512, 768, FFN_ROWS)


def _ffn_body(ie_ref, ib_ref, nv_ref, x_ref, wg_ref, wu_ref, wd_ref, o_ref, *, packed, row_variants):
    it = pl.program_id(0)
    f = pl.program_id(1)
    nv = nv_ref[it]
    d = wg_ref.shape[0]
    gran = row_variants[0]

    @pl.when(jnp.logical_and(f == 0, nv > 0))
    def _():
        o_ref[...] = jnp.zeros_like(o_ref)

    def x_chunk(m, kc):
        if not packed:
            return x_ref[0:m, kc * FFN_KC:(kc + 1) * FFN_KC]
        per_half = (d // 2) // FFN_KC
        c0 = (kc % per_half) * FFN_KC
        w = lax.bitcast_convert_type(x_ref[0:m, c0:c0 + FFN_KC], jnp.uint32)
        bits = (w << 16) if kc < per_half else (w & jnp.uint32(0xFFFF0000))
        return lax.bitcast_convert_type(bits, F32).astype(BF16)

    def compute(m):
        g = u = None
        for kc in range(d // FFN_KC):
            ks = slice(kc * FFN_KC, (kc + 1) * FFN_KC)
            xk = x_chunk(m, kc)
            pg = jnp.dot(xk, wg_ref[ks, :].astype(BF16), preferred_element_type=F32)
            pu = jnp.dot(xk, wu_ref[ks, :].astype(BF16), preferred_element_type=F32)
            g = pg if g is None else g + pg
            u = pu if u is None else u + pu
        a = (jax.nn.silu(g) * u).astype(BF16)
        for nc in range(d // FFN_NC):
            ns = slice(nc * FFN_NC, (nc + 1) * FFN_NC)
            o_ref[0:m, ns] += jnp.dot(a, wd_ref[:, ns].astype(BF16), preferred_element_type=F32)

    n_gran = (nv + gran - 1) // gran
    for m in row_variants:
        @pl.when(n_gran == m // gran)
        def _(m=m):
            compute(m)


def _ffn(xs, wg, wu, wd, lead, item_expert, item_block, item_rows, *, packed, row_variants):
    r = xs.shape[0]
    d = wg.shape[-2]
    n_items = item_expert.shape[0]
    dff = wg.shape[-1]
    n_f = dff // FFN_TF
    assert all(m % row_variants[0] == 0 for m in row_variants) and row_variants[-1] == FFN_ROWS

    def f_idx(it, f, nv):
        return jnp.where(nv[it] > 0, f, n_f - 1)

    if wg.ndim == 4:
        wgu_spec = pl.BlockSpec((None, None, d, FFN_TF), lambda it, f, ie, ib, nv: (lead, ie[it], 0, f_idx(it, f, nv)))
        wd_spec = pl.BlockSpec((None, None, FFN_TF, d), lambda it, f, ie, ib, nv: (lead, ie[it], f_idx(it, f, nv), 0))
    else:
        wgu_spec = pl.BlockSpec((None, d, FFN_TF), lambda it, f, ie, ib, nv: (lead, 0, f_idx(it, f, nv)))
        wd_spec = pl.BlockSpec((None, FFN_TF, d), lambda it, f, ie, ib, nv: (lead, f_idx(it, f, nv), 0))
    grid_spec = pltpu.PrefetchScalarGridSpec(
        num_scalar_prefetch=3,
        grid=(n_items, n_f),
        in_specs=[
            pl.BlockSpec((FFN_ROWS, xs.shape[1]), lambda it, f, ie, ib, nv: (ib[it], 0),
                         pipeline_mode=pl.Buffered(1)),
            wgu_spec, wgu_spec, wd_spec,
        ],
        out_specs=pl.BlockSpec((FFN_ROWS, d), lambda it, f, ie, ib, nv: (ib[it], 0), pipeline_mode=pl.Buffered(1)),
    )
    return pl.pallas_call(
        functools.partial(_ffn_body, packed=packed, row_variants=row_variants),
        grid_spec=grid_spec,
        out_shape=jax.ShapeDtypeStruct((r, d), F32),
        compiler_params=_cparams(("arbitrary", "arbitrary"), vmem=FFN_VMEM_LIMIT),
        name="ffn",
    )(item_expert, item_block, item_rows, xs, wg, wu, wd)


def _moe_dispatch(route, n_tokens):
    e_idx = route[:, :2].astype(jnp.int32)
    flat_e = e_idx.reshape(-1)
    n_pairs = flat_e.shape[0]
    order = jnp.argsort(flat_e, stable=True)
    sorted_e = flat_e[order]
    counts = jnp.sum(flat_e[:, None] == jnp.arange(N_EXPERTS)[None, :], axis=0).astype(jnp.int32)
    blocks_per = (counts + FFN_ROWS - 1) // FFN_ROWS
    blk_end = jnp.cumsum(blocks_per)
    blk_start = blk_end - blocks_per
    grp_start = jnp.cumsum(counts) - counts
    rank = jnp.arange(n_pairs, dtype=jnp.int32) - grp_start[sorted_e]
    dest = blk_start[sorted_e] * FFN_ROWS + rank
    n_items = (n_pairs + N_EXPERTS * (FFN_ROWS - 1)) // FFN_ROWS
    n_rows = n_items * FFN_ROWS
    token_of_row = jnp.zeros((n_rows,), jnp.int32).at[dest].set((order // 2).astype(jnp.int32))
    row_of_pair = jnp.zeros((n_pairs,), jnp.int32).at[order].set(dest).reshape(n_tokens, 2)
    blk = jnp.arange(n_items, dtype=jnp.int32)
    n_real = blk_end[-1]
    blk_c = jnp.minimum(blk, n_real - 1)
    item_e = jnp.minimum(jnp.sum(blk_c[:, None] >= blk_end[None, :], axis=1), N_EXPERTS - 1).astype(jnp.int32)
    rows_left = counts[item_e] - (blk_c - blk_start[item_e]) * FFN_ROWS
    item_rows = jnp.where(blk < n_real, jnp.clip(rows_left, 0, FFN_ROWS), 0).astype(jnp.int32)
    return token_of_row, row_of_pair, item_e, blk_c.astype(jnp.int32), item_rows


def _s5_prep(a_re, a_im, log_dt, b_re, b_im, c_re, c_im, d):
    q, g, p, h = S5_Q, S5_GROUPS, S5_P, S5_H
    dt = jnp.exp(log_dt)[..., None]
    k = jnp.arange(q + 1, dtype=F32)
    mag = jnp.exp((a_re * dt)[..., None] * k)
    ang = (a_im * dt)[..., None] * k
    pw_re, pw_im = mag * jnp.cos(ang), mag * jnp.sin(ang)
    ab_re, ab_im = pw_re[..., 1], pw_im[..., 1]
    den = a_re * a_re + a_im * a_im
    q_re = ((ab_re - 1.0) * a_re + ab_im * a_im) / den
    q_im = (ab_im * a_re - (ab_re - 1.0) * a_im) / den
    bb_re = q_re[..., None] * b_re[None] - q_im[..., None] * b_im[None]
    bb_im = q_re[..., None] * b_im[None] + q_im[..., None] * b_re[None]
    c_re_t = jnp.swapaxes(c_re, 1, 2)[None, ..., None]
    c_im_t = jnp.swapaxes(c_im, 1, 2)[None, ..., None]
    cp_re = c_re_t * pw_re[:, :, :, None, :] - c_im_t * pw_im[:, :, :, None, :]
    cp_im = c_re_t * pw_im[:, :, :, None, :] + c_im_t * pw_re[:, :, :, None, :]
    cpx_re = jnp.swapaxes(cp_re, 3, 4)[..., None]
    cpx_im = jnp.swapaxes(cp_im, 3, 4)[..., None]
    bbx_re = bb_re[:, :, :, None, None, :]
    bbx_im = bb_im[:, :, :, None, None, :]
    kern = jnp.transpose(jnp.sum(cpx_re * bbx_re - cpx_im * bbx_im, axis=2), (0, 2, 1, 3, 4))
    ii = jnp.arange(q)[:, None]
    jj = jnp.arange(q)[None, :]
    lag = ii - jj
    kf = jnp.where((lag >= 0)[..., None, None, None], kern[0][jnp.clip(lag, 0, q)], 0.0)
    kb = jnp.where((lag <= 0)[..., None, None, None], kern[1][jnp.clip(-lag, 0, q)], 0.0)
    m_intra = jnp.transpose(kf + kb, (2, 1, 4, 0, 3)).reshape(g, q * h, q * h)
    pf_re, pf_im = pw_re[0][..., q - 1 - jnp.arange(q)], pw_im[0][..., q - 1 - jnp.arange(q)]
    pb_re, pb_im = pw_re[1][..., :q], pw_im[1][..., :q]

    def contrib(p_re, p_im, b_r, b_i):
        w_re = p_re[:, :, :, None] * b_r[:, :, None, :] - p_im[:, :, :, None] * b_i[:, :, None, :]
        w_im = p_re[:, :, :, None] * b_i[:, :, None, :] + p_im[:, :, :, None] * b_r[:, :, None, :]
        to_rows = lambda w: jnp.transpose(w, (0, 2, 3, 1)).reshape(g, q * h, p)
        return to_rows(w_re), to_rows(w_im)

    wsf_re, wsf_im = contrib(pf_re, pf_im, bb_re[0], bb_im[0])
    wsb_re, wsb_im = contrib(pb_re, pb_im, bb_re[1], bb_im[1])
    w_state = jnp.concatenate([wsf_re, wsf_im, wsb_re, wsb_im], axis=-1)
    to_cols = lambda w: jnp.transpose(w, (0, 1, 3, 2)).reshape(g, p, q * h)
    fi = jnp.arange(q) + 1
    bi = q - jnp.arange(q)
    w_off = jnp.concatenate([to_cols(cp_re[0][..., fi]), to_cols(-cp_im[0][..., fi]),
                             to_cols(cp_re[1][..., bi]), to_cols(-cp_im[1][..., bi])], axis=1)
    aq_re, aq_im = pw_re[..., q], pw_im[..., q]
    a1 = jnp.concatenate([aq_re[0], aq_re[0], aq_re[1], aq_re[1]], axis=-1)[:, None, :]
    a2 = jnp.concatenate([-aq_im[0], aq_im[0], -aq_im[1], aq_im[1]], axis=-1)[:, None, :]
    d_row = jnp.tile(d.reshape(g, 1, h), (1, 1, q))
    return m_intra.astype(BF16), w_state.astype(BF16), w_off.astype(BF16), a1, a2, d_row


def _s5_body(u_ref, mi_ref, ws_ref, wo_ref, a1_ref, a2_ref, d_ref, x0_ref, y_ref, fin_ref, con_ref, sp_ref,
             *, n_ctx_b, n_ctx_c, n_lat_b, n_lat_c):
    half = 2 * S5_P
    ub = u_ref[...]
    u = ub.astype(F32)
    con_ref[...] = jnp.dot(ub, ws_ref[...], preferred_element_type=F32)

    def scan(row0, nb, nc, s_init, lane0, reverse):
        a1 = jnp.broadcast_to(a1_ref[:, lane0:lane0 + half], (nb, half))
        a2 = jnp.broadcast_to(a2_ref[:, lane0:lane0 + half], (nb, half))
        s = s_init
        order = range(nc - 1, -1, -1) if reverse else range(nc)
        for c in order:
            rows = slice(row0 + c * nb, row0 + (c + 1) * nb)
            sp_ref[rows, lane0:lane0 + half] = s
            s = a1 * s + a2 * pltpu.roll(s, S5_P, 1) + con_ref[rows, lane0:lane0 + half]
        return s

    zeros = jnp.zeros((n_ctx_b, half), F32)
    fin_ref[:, 0:half] = scan(0, n_ctx_b, n_ctx_c, zeros, 0, False)
    fin_ref[:, half:2 * half] = scan(0, n_ctx_b, n_ctx_c, zeros, half, True)
    lat0 = n_ctx_b * n_ctx_c
    scan(lat0, n_lat_b, n_lat_c, x0_ref[:, 0:half], 0, False)
    scan(lat0, n_lat_b, n_lat_c, x0_ref[:, half:2 * half], half, True)
    y = jnp.dot(ub, mi_ref[...], preferred_element_type=F32)
    y = y + jnp.dot(sp_ref[...].astype(BF16), wo_ref[...], preferred_element_type=F32)
    y_ref[...] = (y + u * d_ref[...]).astype(y_ref.dtype)


def _s5_scan(u_rows, ops, x0, n_ctx_b, n_ctx_c, n_lat_b, n_lat_c):
    g, r, w = u_rows.shape
    m_intra, w_state, w_off, a1, a2, d_row = ops
    mat = pl.BlockSpec((None, w, w), lambda i: (i, 0, 0))
    row = pl.BlockSpec((None, 1, w), lambda i: (i, 0, 0))
    return pl.pallas_call(
        functools.partial(_s5_body, n_ctx_b=n_ctx_b, n_ctx_c=n_ctx_c, n_lat_b=n_lat_b, n_lat_c=n_lat_c),
        grid=(g,),
        in_specs=[pl.BlockSpec((None, r, w), lambda i: (i, 0, 0)), mat, mat, mat, row, row, row,
                  pl.BlockSpec((None, n_lat_b, w), lambda i: (i, 0, 0))],
        out_specs=[pl.BlockSpec((None, r, w), lambda i: (i, 0, 0)),
                   pl.BlockSpec((None, n_ctx_b, w), lambda i: (i, 0, 0))],
        out_shape=[jax.ShapeDtypeStruct((g, r, w), BF16), jax.ShapeDtypeStruct((g, n_ctx_b, w), F32)],
        scratch_shapes=[pltpu.VMEM((r, w), F32), pltpu.VMEM((r, w), F32)],
        compiler_params=_cparams(("arbitrary",)),
        name="s5_scan",
    )(u_rows, m_intra, w_state, w_off, a1, a2, d_row, x0)


def _s5_glu_body(y_ref, w_ref, o_ref, wbf_ref):
    @pl.when(pl.program_id(0) == 0)
    def _():
        wbf_ref[...] = w_ref[...].astype(BF16)

    y = jax.nn.gelu(y_ref[...].astype(F32))
    z = jnp.dot(y.astype(BF16), wbf_ref[...], preferred_element_type=F32)
    o_ref[...] = (y * jax.nn.sigmoid(z)).astype(o_ref.dtype)


def _s5_glu(y, w_glu, lead, tm=512):
    m, d = y.shape
    return pl.pallas_call(
        _s5_glu_body,
        grid=(m // tm,),
        in_specs=[pl.BlockSpec((tm, d), lambda i: (i, 0)), pl.BlockSpec((None, d, d), lambda i: (lead, 0, 0))],
        out_specs=pl.BlockSpec((tm, d), lambda i: (i, 0)),
        out_shape=jax.ShapeDtypeStruct((m, d), BF16),
        scratch_shapes=[pltpu.VMEM((d, d), BF16)],
        compiler_params=_cparams(("arbitrary",)),
        name="s5_glu",
    )(y, w_glu)


def _conv3(u, w_ref, b_ref, seg):
    n = u.shape[0]
    assert seg & (seg - 1) == 0
    pos = lax.broadcasted_iota(jnp.int32, u.shape, 0) & (seg - 1)
    prev = jnp.where(pos == 0, 0.0, pltpu.roll(u, 1, 0))
    nxt = jnp.where(pos == seg - 1, 0.0, pltpu.roll(u, n - 1, 0))
    return b_ref[...] + w_ref[0:1, :] * prev + w_ref[1:2, :] * u + w_ref[2:3, :] * nxt


def _softplus(x):
    return jnp.maximum(x, 0.0) + jnp.log(1.0 + jnp.exp(-jnp.abs(x)))


def _ssd_body(*refs, seq, seg, has_init, want_fin, n_aliased):
    (x_ref, b_ref, c_ref, dt_ref, cwx_ref, cwb_ref, cwc_ref, cbx_ref, cbb_ref, cbc_ref,
     dtb_ref, arow_ref, drow_ref) = refs[:13]
    k = 13
    init_ref = None
    if has_init:
        init_ref = refs[k]
        k += 1
    k += n_aliased
    y_ref = refs[k]
    k += 1
    fin_ref = None
    if want_fin:
        fin_ref = refs[k]
        k += 1
    xc_ref, bc_ref, cc_ref, stf_ref, stb_ref = refs[k:k + 5]
    q, p, hpg = SSD_CHUNK, SSD_HEADDIM, SSD_HPG
    wid = hpg * p
    nc = seq // q

    xc_ref[...] = jax.nn.silu(_conv3(x_ref[...], cwx_ref, cbx_ref, seg))
    bc_ref[...] = jax.nn.silu(_conv3(b_ref[...], cwb_ref, cbb_ref, seg))
    cc_ref[...] = jax.nn.silu(_conv3(c_ref[...], cwc_ref, cbc_ref, seg))
    if has_init:
        stf_ref[...] = jnp.concatenate([init_ref[0, r] for r in range(hpg)], axis=0).T
        stb_ref[...] = jnp.concatenate([init_ref[1, r] for r in range(hpg)], axis=0).T
    else:
        stf_ref[...] = jnp.zeros_like(stf_ref)
        stb_ref[...] = jnp.zeros_like(stb_ref)

    ii = lax.broadcasted_iota(jnp.int32, (q, q), 0)
    jj = lax.broadcasted_iota(jnp.int32, (q, q), 1)
    tri_lo = (ii >= jj).astype(F32)
    tri_up = (ii <= jj).astype(F32)
    er = lax.broadcasted_iota(jnp.int32, (LANES, wid), 0)
    ec = lax.broadcasted_iota(jnp.int32, (LANES, wid), 1) // p
    exp_f = (er == ec).astype(F32)
    exp_b = (er == ec + hpg).astype(F32)
    neg = jnp.float32(-jnp.inf)
    hdot = functools.partial(jnp.dot, preferred_element_type=F32, precision=HIGHEST)
    bdot = functools.partial(jnp.dot, preferred_element_type=F32)

    def chunk_dt(r0):
        dt = _softplus(dt_ref[pl.ds(r0, q), :] + dtb_ref[...])
        return dt, dt * arow_ref[...]

    def fwd(c, carry):
        r0 = pl.multiple_of(c * q, q)
        dt, adt = chunk_dt(r0)
        cs = hdot(tri_lo, adt)
        rcs = hdot(tri_up, adt)
        cs_t, rcs_t = cs.T, rcs.T
        x = xc_ref[pl.ds(r0, q), :]
        bm = bc_ref[pl.ds(r0, q), :]
        cm = cc_ref[pl.ds(r0, q), :].astype(BF16)
        cb = lax.dot_general(cm, bm.astype(BF16), (((1,), (1,)), ((), ())), preferred_element_type=F32)
        xs_f = x * hdot(dt, exp_f)
        xs_b = x * hdot(dt, exp_b)
        csx = hdot(cs, exp_f)
        parts = []
        for r in range(hpg):
            cols = slice(r * p, (r + 1) * p)
            lf = jnp.exp(jnp.where(ii >= jj, cs[:, r:r + 1] - cs_t[r:r + 1, :], neg))
            lb = jnp.exp(jnp.where(jj >= ii, rcs[:, hpg + r:hpg + r + 1] - rcs_t[hpg + r:hpg + r + 1, :], neg))
            lhs = jnp.concatenate([cb * lf, cb * lb], axis=1).astype(BF16)
            rhs = jnp.concatenate([xs_f[:, cols], xs_b[:, cols]], axis=0).astype(BF16)
            parts.append(bdot(lhs, rhs))
        y = jnp.concatenate(parts, axis=1)
        st = stf_ref[...]
        y = y + bdot(cm, st.astype(BF16)) * jnp.exp(csx) + drow_ref[...] * x
        y_ref[pl.ds(r0, q), :] = y
        tot = csx[q - 1:q, :]
        dec = jnp.exp(tot - csx)
        stf_ref[...] = st * jnp.exp(tot) + bdot(bm.T.astype(BF16), (xs_f * dec).astype(BF16))
        return carry

    def bwd(k_it, carry):
        c = nc - 1 - k_it
        r0 = pl.multiple_of(c * q, q)
        dt, adt = chunk_dt(r0)
        rcsx = hdot(hdot(tri_up, adt), exp_b)
        x = xc_ref[pl.ds(r0, q), :]
        bm = bc_ref[pl.ds(r0, q), :]
        cm = cc_ref[pl.ds(r0, q), :].astype(BF16)
        xs_b = x * hdot(dt, exp_b)
        st = stb_ref[...]
        y_ref[pl.ds(r0, q), :] += bdot(cm, st.astype(BF16)) * jnp.exp(rcsx)
        tot = rcsx[0:1, :]
        dec = jnp.exp(tot - rcsx)
        stb_ref[...] = st * jnp.exp(tot) + bdot(bm.T.astype(BF16), (xs_b * dec).astype(BF16))
        return carry

    lax.fori_loop(0, nc, fwd, 0)
    lax.fori_loop(0, nc, bwd, 0)
    if want_fin:
        fin_f, fin_b = stf_ref[...].T, stb_ref[...].T
        for r in range(hpg):
            fin_ref[0, r] = fin_f[r * p:(r + 1) * p, :]
            fin_ref[1, r] = fin_b[r * p:(r + 1) * p, :]


def _ssd_scan(proj, dtg, conv_w, conv_b, dtb_rows, a_rows, d_rows, init, y_prev, fin_prev, *, layer, row0, n_b,
              seq, seg, want_fin):
    q, p, hpg, ng, ns = SSD_CHUNK, SSD_HEADDIM, SSD_HPG, SSD_NGROUPS, SSD_DSTATE
    wid = hpg * p
    rb0 = row0 // seq
    xcol0 = OFF_XBC // wid
    bcol0 = (OFF_XBC + D_SSD) // ns
    ccol0 = bcol0 + ng
    in_specs = [
        pl.BlockSpec((seq, wid), lambda b, g: (rb0 + b, xcol0 + g)),
        pl.BlockSpec((seq, ns), lambda b, g: (rb0 + b, bcol0 + g)),
        pl.BlockSpec((seq, ns), lambda b, g: (rb0 + b, ccol0 + g)),
        pl.BlockSpec((None, seq, LANES), lambda b, g: (g, rb0 + b, 0)),
        pl.BlockSpec((3, wid), lambda b, g: (0, g)),
        pl.BlockSpec((3, ns), lambda b, g: (0, D_SSD // ns + g)),
        pl.BlockSpec((3, ns), lambda b, g: (0, D_SSD // ns + ng + g)),
        pl.BlockSpec((1, wid), lambda b, g: (0, g)),
        pl.BlockSpec((1, ns), lambda b, g: (0, D_SSD // ns + g)),
        pl.BlockSpec((1, ns), lambda b, g: (0, D_SSD // ns + ng + g)),
        pl.BlockSpec((None, 1, LANES), lambda b, g: (g, 0, 0)),
        pl.BlockSpec((None, 1, LANES), lambda b, g: (g, 0, 0)),
        pl.BlockSpec((None, 1, wid), lambda b, g: (g, 0, 0)),
    ]
    args = [proj, proj, proj, dtg, conv_w, conv_w, conv_w, conv_b, conv_b, conv_b, dtb_rows, a_rows, d_rows]
    if init is not None:
        in_specs.append(pl.BlockSpec((None, 2, hpg, p, ns), lambda b, g: (b, 0, g, 0, 0)))
        args.append(init)
    aliases = {}
    for k_out, prev in enumerate((y_prev, fin_prev)):
        if prev is not None:
            in_specs.append(pl.BlockSpec(memory_space=pl.ANY))
            args.append(prev)
            aliases[len(args) - 1] = k_out
    out_specs = [pl.BlockSpec((seq, wid), lambda b, g: (rb0 + b, g))]
    out_shape = [jax.ShapeDtypeStruct((proj.shape[0], D_SSD), F32)]
    if want_fin:
        out_specs.append(pl.BlockSpec((None, None, 2, hpg, p, ns), lambda b, g: (b, layer, 0, g, 0, 0)))
        out_shape.append(jax.ShapeDtypeStruct((n_b, DEPTH, 2, SSD_HEADS, p, ns), F32))
    else:
        assert fin_prev is None
    res = pl.pallas_call(
        functools.partial(_ssd_body, seq=seq, seg=seg, has_init=init is not None, want_fin=want_fin,
                          n_aliased=len(aliases)),
        grid=(n_b, ng),
        in_specs=in_specs,
        out_specs=out_specs,
        out_shape=out_shape,
        input_output_aliases=aliases,
        scratch_shapes=[pltpu.VMEM((seq, wid), F32), pltpu.VMEM((seq, ns), F32), pltpu.VMEM((seq, ns), F32),
                        pltpu.VMEM((ns, wid), F32), pltpu.VMEM((ns, wid), F32)],
        compiler_params=_cparams(("arbitrary", "arbitrary")),
        name="ssd_scan",
    )(*args)
    return res if want_fin else (res[0], None)


def _ssd_norm_body(y_ref, z0_ref, z1_ref, w_ref, o_ref):
    z = jnp.concatenate([z0_ref[...], z1_ref[...]], axis=1)
    y = y_ref[...] * jax.nn.silu(z)
    ms = jnp.mean(y * y, axis=-1, keepdims=True)
    o_ref[...] = (y * lax.rsqrt(ms + EPS) * w_ref[...]).astype(o_ref.dtype)


def _ssd_norm(y, proj, norm_w, tm=256):
    m, d = y.shape
    half = d // 2
    zb = OFF_Z // half
    return pl.pallas_call(
        _ssd_norm_body,
        grid=(m // tm,),
        in_specs=[pl.BlockSpec((tm, d), lambda i: (i, 0)),
                  pl.BlockSpec((tm, half), lambda i: (i, zb)),
                  pl.BlockSpec((tm, half), lambda i: (i, zb + 1)),
                  pl.BlockSpec((1, d), lambda i: (0, 0))],
        out_specs=pl.BlockSpec((tm, d), lambda i: (i, 0)),
        out_shape=jax.ShapeDtypeStruct((m, d), BF16),
        compiler_params=_cparams(("arbitrary",)),
        name="ssd_norm",
    )(y, proj, proj, norm_w)


def _hyena_filters(seq, w1, b1, w2, b2, w3, freq):
    t = jnp.arange(seq, dtype=F32) / seq
    bands = jnp.linspace(1e-4, HY_BANDS - 1, HY_BANDS, dtype=F32)
    ang = 2.0 * math.pi * t[:, None] * bands[None, :]
    feat = jnp.concatenate([t[:, None], jnp.cos(ang), -jnp.sin(ang)], axis=-1)
    dot = functools.partial(jnp.dot, precision=HIGHEST)
    hdn = jnp.sin(freq[0] * (dot(feat, w1) + b1))
    hdn = jnp.sin(freq[1] * (dot(hdn, w2) + b2))
    filt = dot(hdn, w3).reshape(seq, 2, HY_ORDER, D_HY)
    deltas = jnp.abs(jnp.linspace(HY_MIN_DECAY, HY_MAX_DECAY, D_HY, dtype=F32))
    filt = filt * jnp.exp(-t[:, None] * deltas[None, :])[:, None, None, :]
    hf = filt[:, 0].reshape(seq, HY_ORDER * D_HY)
    hb = filt[:, 1].reshape(seq, HY_ORDER * D_HY)
    return hf + hb, hf - hb


def _split_bf16(x):
    hi = x.astype(BF16)
    return hi, (x - hi.astype(F32)).astype(BF16)


def _dft_ops(seq):
    n = 2 * seq
    f = jnp.arange(seq, dtype=jnp.int32)[:, None]
    t = jnp.arange(seq, dtype=jnp.int32)[None, :]
    ang = ((f * t) % n).astype(F32) * (math.pi / seq)
    cos_ft, sin_ft = jnp.cos(ang), jnp.sin(ang)
    nyq = jnp.where(t % 2 == 0, 1.0, -1.0).astype(F32)
    fwd = jnp.concatenate([cos_ft, nyq, -sin_ft[1:]], axis=0)
    wgt = jnp.where(f == 0, 1.0, 2.0).astype(F32) / n
    inv = jnp.concatenate([(wgt * cos_ft).T, nyq.T / n, (-(2.0 / n) * sin_ft[1:]).T], axis=1)
    return _split_bf16(fwd) + _split_bf16(inv)


def _dot3(a_hi, a_lo, b_hi, b_lo):
    d = functools.partial(jnp.dot, preferred_element_type=F32)
    return d(a_hi, b_hi) + (d(a_hi, b_lo) + d(a_lo, b_hi))


def _spectra_body(hs_ref, hd_ref, fh_ref, fl_ref, kr_ref, ki_ref, kn_ref, *, seq):
    s_hi, s_lo = _split_bf16(hs_ref[...])
    d_hi, d_lo = _split_bf16(hd_ref[...])
    fs = _dot3(fh_ref[...], fl_ref[...], s_hi, s_lo)
    fd = _dot3(fh_ref[...], fl_ref[...], d_hi, d_lo)
    row0 = lax.broadcasted_iota(jnp.int32, (seq, 1), 0) == 0
    kr = fs[:seq]
    kr_ref[...] = kr
    kn_ref[...] = jnp.where(row0, fs[seq:seq + 1], kr)
    ki_ref[...] = jnp.where(row0, 0.0, fd[seq:])


def _hyena_spectra(hsum, hdiff, dft, seq, tc=256):
    fh, fl = dft[0], dft[1]
    n = hsum.shape[1]
    blk = pl.BlockSpec((seq, tc), lambda cb: (0, cb))
    const = pl.BlockSpec((2 * seq, seq), lambda cb: (0, 0), pipeline_mode=pl.Buffered(1))
    return pl.pallas_call(
        functools.partial(_spectra_body, seq=seq),
        grid=(n // tc,),
        in_specs=[blk, blk, const, const],
        out_specs=[blk, blk, blk],
        out_shape=[jax.ShapeDtypeStruct((seq, n), F32)] * 3,
        compiler_params=_cparams(("arbitrary",)),
        name="hyena_spectra",
    )(hsum, hdiff, fh, fl)


def _hyena_body(*refs, seq, seg, aliased):
    (v_ref, x1_ref, x2_ref, cwv_ref, cw1_ref, cw2_ref, cbv_ref, cb1_ref, cb2_ref,
     kr0_ref, kr1_ref, ki0_ref, ki1_ref, kn0_ref, kn1_ref, bias_ref, fh_ref, fl_ref, gh_ref, gl_ref) = refs[:20]
    o_ref = refs[-1]
    kr, ki, kn = (kr0_ref, kr1_ref), (ki0_ref, ki1_ref), (kn0_ref, kn1_ref)
    z = _conv3(v_ref[...], cwv_ref, cbv_ref, seg)
    gates = (_conv3(x1_ref[...], cw1_ref, cb1_ref, seg), _conv3(x2_ref[...], cw2_ref, cb2_ref, seg))
    for o in range(HY_ORDER):
        z_hi, z_lo = _split_bf16(z)
        zf = _dot3(fh_ref[...], fl_ref[...], z_hi, z_lo)
        a, b = zf[:seq], zf[seq:]
        pre = a * kr[o][...] - b * ki[o][...]
        pim = a * ki[o][...] + b * kn[o][...]
        p_hi, p_lo = _split_bf16(jnp.concatenate([pre, pim], axis=0))
        conv = _dot3(gh_ref[...], gl_ref[...], p_hi, p_lo)
        z = gates[o] * (conv + bias_ref[o:o + 1, :] * z)
    o_ref[...] = z.astype(o_ref.dtype)


def _hyena(proj, conv_w, conv_b, spectra, bias, dft, out_prev, *, n_rows, row0, n_b, seq, seg, tc):
    assert HY_ORDER == 2
    kr, ki, kn = spectra
    fh, fl, gh, gl = dft
    rb0 = row0 // seq
    ncb = D_HY // tc
    assert PROJ_HY % tc == 0
    u_spec = lambda part: pl.BlockSpec((seq, tc), lambda cb, b: (rb0 + b, PROJ_HY // tc + part * ncb + cb))
    w_spec = lambda part: pl.BlockSpec((3, tc), lambda cb, b: (0, part * ncb + cb))
    b_spec = lambda part: pl.BlockSpec((1, tc), lambda cb, b: (0, part * ncb + cb))
    k_spec = lambda o: pl.BlockSpec((seq, tc), lambda cb, b: (0, o * ncb + cb))
    const = lambda shape: pl.BlockSpec(shape, lambda cb, b: (0, 0), pipeline_mode=pl.Buffered(1))
    in_specs = [u_spec(0), u_spec(1), u_spec(2), w_spec(0), w_spec(1), w_spec(2), b_spec(0), b_spec(1), b_spec(2),
                k_spec(0), k_spec(1), k_spec(0), k_spec(1), k_spec(0), k_spec(1),
                pl.BlockSpec((HY_ORDER, tc), lambda cb, b: (0, cb)),
                const((2 * seq, seq)), const((2 * seq, seq)), const((seq, 2 * seq)), const((seq, 2 * seq))]
    args = [proj, proj, proj, conv_w, conv_w, conv_w, conv_b, conv_b, conv_b, kr, kr, ki, ki, kn, kn, bias,
            fh, fl, gh, gl]
    aliases = {}
    if out_prev is not None:
        in_specs.append(pl.BlockSpec(memory_space=pl.ANY))
        args.append(out_prev)
        aliases = {len(args) - 1: 0}
    return pl.pallas_call(
        functools.partial(_hyena_body, seq=seq, seg=seg, aliased=out_prev is not None),
        grid=(ncb, n_b),
        in_specs=in_specs,
        out_specs=pl.BlockSpec((seq, tc), lambda cb, b: (rb0 + b, cb)),
        out_shape=jax.ShapeDtypeStruct((n_rows, D_HY), BF16),
        input_output_aliases=aliases,
        compiler_params=_cparams(("arbitrary", "arbitrary")),
        name="hyena",
    )(*args)


def _s5_mixer(proj, p, ops, l, dims):
    n_cb, n_cs, n_lb, n_ls = dims
    t_ctx = n_cb * n_cs
    nq_c, nq_l = n_cs // S5_Q, n_ls // S5_Q
    u = lax.optimization_barrier(proj[:, :D_S5]).astype(BF16)

    def to_rows(a, nb, nq):
        a = jnp.transpose(a.reshape(nb, nq, S5_Q, S5_GROUPS, S5_H), (3, 1, 0, 2, 4))
        return a.reshape(S5_GROUPS, nq * nb, S5_Q * S5_H)

    def from_rows(a, nb, nq):
        a = jnp.transpose(a.reshape(S5_GROUPS, nq, nb, S5_Q, S5_H), (2, 1, 3, 0, 4))
        return a.reshape(nb * nq * S5_Q, D_S5)

    u_rows = jnp.concatenate([to_rows(u[:t_ctx], n_cb, nq_c), to_rows(u[t_ctx:], n_lb, nq_l)], axis=1)
    sre, sim = p['state_s5_re'][:, l], p['state_s5_im'][:, l]
    x0 = jnp.transpose(jnp.concatenate([sre[:, 0], sim[:, 0], sre[:, 1], sim[:, 1]], axis=-1), (1, 0, 2))
    y_rows, fin = _s5_scan(u_rows, ops, x0, n_cb, nq_c, n_lb, nq_l)
    y = jnp.concatenate([from_rows(y_rows[:, :nq_c * n_cb], n_cb, nq_c),
                         from_rows(y_rows[:, nq_c * n_cb:], n_lb, nq_l)], axis=0)
    y = _s5_glu(y, p['s5_w_glu'], l)
    fin = jnp.transpose(fin.reshape(S5_GROUPS, n_cb, 4, S5_P), (1, 2, 0, 3))
    return y, fin[:, 0::2], fin[:, 1::2]


def _ssd_mixer(proj, dt_raw, p, l, dims, fin_prev):
    n_cb, n_cs, n_lb, n_ls = dims
    t_ctx = n_cb * n_cs
    t_all = proj.shape[0]
    pad_heads = jnp.zeros((SSD_NGROUPS, LANES - 2 * SSD_HPG), F32)

    def per_group(a):
        return jnp.concatenate([a[0].reshape(SSD_NGROUPS, SSD_HPG), a[1].reshape(SSD_NGROUPS, SSD_HPG),
                                pad_heads], axis=1)[:, None, :]

    dtb_rows = per_group(p['ssd_dt_bias'][l])
    a_rows = per_group(-jnp.exp(p['ssd_a_log'][l]))
    d_rows = jnp.repeat(p['ssd_d'][l], SSD_HEADDIM).reshape(SSD_NGROUPS, 1, SSD_HPG * SSD_HEADDIM)
    dtr = dt_raw.reshape(t_all, 2, SSD_NGROUPS, SSD_HPG)
    dtg = jnp.concatenate([jnp.transpose(dtr, (2, 0, 1, 3)).reshape(SSD_NGROUPS, t_all, 2 * SSD_HPG),
                           jnp.zeros((SSD_NGROUPS, t_all, LANES - 2 * SSD_HPG), F32)], axis=-1)
    cw, cb = p['ssd_conv_w'][l], p['ssd_conv_b'][l][None]
    y, fin = _ssd_scan(proj, dtg, cw, cb, dtb_rows, a_rows, d_rows, None, None, fin_prev,
                       layer=l, row0=0, n_b=n_cb, seq=n_cs, seg=n_cs, want_fin=True)
    y, _ = _ssd_scan(proj, dtg, cw, cb, dtb_rows, a_rows, d_rows, p['state_ssd'][:, l], y, None,
                     layer=l, row0=t_ctx, n_b=n_lb, seq=n_ls, seg=GRID_W, want_fin=False)
    return _ssd_norm(y, proj, p['ssd_norm'][l][None]), fin


def _hyena_mixer(proj, p, l, dims, dfts):
    n_cb, n_cs, n_lb, n_ls = dims
    hcw, hcb = p['hy_conv_w'][l], p['hy_conv_b'][l][None]
    out = None
    cfgs = ((0, n_cb, n_cs, n_cs, 512), (n_cb * n_cs, n_lb, n_ls, GRID_W, 256))
    for (row0, nb, seq, seg, tc), dft in zip(cfgs, dfts):
        hsum, hdiff = _hyena_filters(seq, p['hy_w1'][l], p['hy_b1'][l], p['hy_w2'][l], p['hy_b2'][l], p['hy_w3'][l],
                                     p['hy_freq'][l])
        out = _hyena(proj, hcw, hcb, _hyena_spectra(hsum, hdiff, dft, seq), p['hy_bias'][l], dft, out,
                     n_rows=proj.shape[0], row0=row0, n_b=nb, seq=seq, seg=seg, tc=tc)
    return out


def _pack_w_in(w_in):
    pad = jnp.zeros(w_in.shape[:2] + (N_PROJ - N_IN,), F32)
    return jnp.concatenate([w_in[:, :, :OFF_DT], w_in[:, :, OFF_HY:], w_in[:, :, OFF_DT:OFF_HY], pad], axis=2)


def _moe(h_packed, route, wg, wu, wd, lead):
    t_all = h_packed.shape[0]
    token_of_row, row_of_pair, item_e, item_b, item_rows = _moe_dispatch(route, t_all)
    xs = jnp.take(h_packed, token_of_row, axis=0, mode="clip")
    ys = _ffn(xs, wg, wu, wd, lead, item_e, item_b, item_rows, packed=True, row_variants=FFN_MOE_ROWS)
    return (jnp.take(ys, row_of_pair[:, 0], axis=0, mode="clip"),
            jnp.take(ys, row_of_pair[:, 1], axis=0, mode="clip"))


def kernel(x_prompt, x_sample, state_s5_re, state_s5_im, state_ssd, c, c_ctx, ada_w, ada_b, norm1, norm2, final_norm, w_in, w_out, s5_a_re, s5_a_im, s5_log_dt, s5_b_re, s5_b_im, s5_c_re, s5_c_im, s5_d, s5_w_glu, ssd_conv_w, ssd_conv_b, ssd_dt_bias, ssd_a_log, ssd_d, ssd_norm, hy_conv_w, hy_conv_b, hy_w1, hy_b1, hy_w2, hy_b2, hy_w3, hy_freq, hy_bias, ffn_wg, ffn_wu, ffn_wd, moe_router, moe_wg, moe_wu, moe_wd):
    n_cb, n_cs, d = x_prompt.shape
    n_lb, n_ls, _ = x_sample.shape
    assert d == D_MODEL and n_ls == LAT_ROWS and n_cs % SSD_CHUNK == 0 and n_ls % SSD_CHUNK == 0
    t_ctx, t_lat = n_cb * n_cs, n_lb * n_ls
    t_all = t_ctx + t_lat
    assert t_ctx % LAT_ROWS == 0 and t_all % FFN_ROWS == 0 and 1 + n_lb <= 3
    x = jnp.concatenate([x_prompt.reshape(t_ctx, d), x_sample.reshape(t_lat, d)], axis=0)

    cond = jnp.zeros((8, d), F32).at[0].set(c_ctx).at[1:1 + n_lb].set(c)
    mod = jnp.concatenate(
        [_matmul(cond, ada_w, l, 6 * d, tm=8, tn=512, out_dtype=F32, a_silu=True,
                 bias=ada_b.reshape(DEPTH, 1, 6 * d), name="ada_mod")[None] for l in range(DEPTH)], axis=0)
    mod3 = mod[:, :3].reshape(DEPTH, 3, 1, 6 * d)

    p = dict(state_s5_re=state_s5_re, state_s5_im=state_s5_im, state_ssd=state_ssd,
             s5_a_re=s5_a_re, s5_a_im=s5_a_im, s5_log_dt=s5_log_dt, s5_b_re=s5_b_re, s5_b_im=s5_b_im,
             s5_c_re=s5_c_re, s5_c_im=s5_c_im, s5_d=s5_d, s5_w_glu=s5_w_glu,
             ssd_conv_w=ssd_conv_w, ssd_conv_b=ssd_conv_b, ssd_dt_bias=ssd_dt_bias, ssd_a_log=ssd_a_log,
             ssd_d=ssd_d, ssd_norm=ssd_norm, hy_conv_w=hy_conv_w, hy_conv_b=hy_conv_b, hy_w1=hy_w1, hy_b1=hy_b1,
             hy_w2=hy_w2, hy_b2=hy_b2, hy_w3=hy_w3, hy_freq=hy_freq, hy_bias=hy_bias)
    dims = (n_cb, n_cs, n_lb, n_ls)
    w_proj = _pack_w_in(w_in)
    dfts = (_dft_ops(n_cs), _dft_ops(n_ls))
    s5_ops = jax.vmap(_s5_prep)(s5_a_re, s5_a_im, s5_log_dt, s5_b_re, s5_b_im, s5_c_re, s5_c_im, s5_d)
    fin_re, fin_im, fin_ssd = [], [], None
    for l in range(DEPTH):
        m3 = mod3[l]
        h = _norm_mod(x, norm1[l][None], m3, 1, 0, t_ctx)
        proj = _matmul(h, w_proj, l, N_PROJ, tm=1024, tn=PROJ_TN, out_dtype=F32, name="w_in")
        dt_raw = proj[:, PROJ_DT:PROJ_DT + 2 * SSD_HEADS]
        y_s5, f_re, f_im = _s5_mixer(proj, p, tuple(o[l] for o in s5_ops), l, dims)
        y_ssd, fin_ssd = _ssd_mixer(proj, dt_raw, p, l, dims, fin_ssd)
        y_hy = _hyena_mixer(proj, p, l, dims, dfts)
        fin_re.append(f_re)
        fin_im.append(f_im)
        x = _matmul([y_s5, y_ssd, y_hy], w_out, l, d, tm=1024, tn=512, out_dtype=F32, resid=x,
                    gate=m3[:, :, 2 * d:3 * d], n_ctx_rows=t_ctx, name="w_out")
        final_w = final_norm[None] if l == DEPTH - 1 else None
        if l % 2 == 0:
            h = _norm_mod(x, norm2[l][None], m3, 4, 3, t_ctx)
            n_items = t_all // FFN_ROWS
            f = _ffn(h, ffn_wg, ffn_wu, ffn_wd, l // 2, jnp.zeros((n_items,), jnp.int32),
                     jnp.arange(n_items, dtype=jnp.int32), jnp.full((n_items,), FFN_ROWS, jnp.int32),
                     packed=False, row_variants=FFN_DENSE_ROWS)
            x = _resid(x, [f], m3, 5, t_ctx, final_w=final_w)
        else:
            router = jnp.concatenate([moe_router[l // 2], jnp.zeros((d, LANES - N_EXPERTS), F32)], axis=1)
            h_packed, route = _norm_mod(x, norm2[l][None], m3, 4, 3, t_ctx, router=router)
            x = _resid(x, _moe(h_packed, route, moe_wg, moe_wu, moe_wd, l // 2), m3, 5, t_ctx, route=route,
                       final_w=final_w)

    y_prompt = x[:t_ctx].reshape(n_cb, n_cs, d)
    y_sample = x[t_ctx:].reshape(n_lb, n_ls, d)
    return (y_prompt, y_sample, jnp.stack(fin_re, axis=1), jnp.stack(fin_im, axis=1), fin_ssd)
```

```python
import functools
import math

import jax
import jax.numpy as jnp
from jax import lax
from jax.experimental import pallas as pl
from jax.experimental.pallas import tpu as pltpu

F32 = jnp.float32
BF16 = jnp.bfloat16
HIGHEST = lax.Precision.HIGHEST

D_MODEL = 4096
DEPTH = 2
GRID_W = 64
EPS = 1e-6
D_S5 = D_MODEL // 4
D_SSD = D_MODEL // 2
D_HY = D_MODEL - D_S5 - D_SSD
S5_H = 16
S5_GROUPS = D_S5 // S5_H
S5_P = 64
S5_Q = 16
SSD_HEADDIM = 64
SSD_HEADS = D_SSD // SSD_HEADDIM
SSD_NGROUPS = 4
SSD_HPG = SSD_HEADS // SSD_NGROUPS
SSD_DSTATE = 128
SSD_CHUNK = 128
SSD_CONV_DIM = D_SSD + 2 * SSD_NGROUPS * SSD_DSTATE
HY_ORDER = 2
HY_BANDS = 16
HY_MIN_DECAY = math.log(1e-2) / 1.5
HY_MAX_DECAY = math.log(1e-2) / 0.3
D_FF = 14336
N_EXPERTS = 8
OFF_Z = D_S5
OFF_XBC = OFF_Z + D_SSD
OFF_DT = OFF_XBC + SSD_CONV_DIM
OFF_HY = OFF_DT + 2 * SSD_HEADS
N_IN = OFF_HY + (HY_ORDER + 1) * D_HY

LANES = 128
VMEM_LIMIT = 56 * 1024 * 1024
LAT_ROWS = 1024
PROJ_TN = 512
N_PROJ = -(-N_IN // PROJ_TN) * PROJ_TN
PROJ_HY = OFF_DT
PROJ_DT = PROJ_HY + (HY_ORDER + 1) * D_HY


def _cparams(sem, vmem=VMEM_LIMIT):
    return pltpu.CompilerParams(dimension_semantics=sem, vmem_limit_bytes=vmem)


def _mod_row(i, tm, n_ctx_rows):
    return jnp.maximum((i * tm - n_ctx_rows) // LAT_ROWS + 1, 0)


def _mm_body(*refs, n_a, a_silu, has_bias, has_resid):
    a_refs, w_ref = refs[:n_a], refs[n_a]
    k = n_a + 1
    bias_ref = resid_ref = gate_ref = None
    if has_bias:
        bias_ref = refs[k]
        k += 1
    if has_resid:
        resid_ref, gate_ref = refs[k], refs[k + 1]
        k += 2
    o_ref, wbf_ref = refs[k], refs[k + 1]

    @pl.when(pl.program_id(1) == 0)
    def _():
        wbf_ref[...] = w_ref[...].astype(BF16)

    acc = None
    k0 = 0
    for a_ref in a_refs:
        a = a_ref[...]
        if a_silu:
            a = jax.nn.silu(a).astype(BF16)
        part = jnp.dot(a, wbf_ref[k0:k0 + a.shape[1], :], preferred_element_type=F32)
        acc = part if acc is None else acc + part
        k0 += a.shape[1]
    if has_bias:
        acc = acc + bias_ref[...]
    if has_resid:
        acc = resid_ref[...] + gate_ref[...] * acc
    o_ref[...] = acc.astype(o_ref.dtype)


def _matmul(a, w, w_lead, n_cols, *, tm, tn, out_dtype, a_silu=False, bias=None, resid=None, gate=None,
            n_ctx_rows=0, name="matmul"):
    a_list = list(a) if isinstance(a, (list, tuple)) else [a]
    m = a_list[0].shape[0]
    kdim = sum(x.shape[1] for x in a_list)
    assert m % tm == 0 and n_cols % tn == 0 and kdim == w.shape[-2]
    grid = (n_cols // tn, m // tm)
    w_spec = pl.BlockSpec((None, kdim, tn), lambda j, i: (w_lead, 0, j))
    in_specs = [pl.BlockSpec((tm, x.shape[1]), lambda j, i: (i, 0)) for x in a_list] + [w_spec]
    args = a_list + [w]
    if bias is not None:
        in_specs.append(pl.BlockSpec((None, 1, tn), lambda j, i: (w_lead, 0, j)))
        args.append(bias)
    if resid is not None:
        in_specs.append(pl.BlockSpec((tm, tn), lambda j, i: (i, j)))
        in_specs.append(pl.BlockSpec((None, 1, tn), lambda j, i: (_mod_row(i, tm, n_ctx_rows), 0, j)))
        args += [resid, gate]
    return pl.pallas_call(
        functools.partial(_mm_body, n_a=len(a_list), a_silu=a_silu, has_bias=bias is not None,
                          has_resid=resid is not None),
        grid=grid,
        in_specs=in_specs,
        out_specs=pl.BlockSpec((tm, tn), lambda j, i: (i, j)),
        out_shape=jax.ShapeDtypeStruct((m, n_cols), out_dtype),
        scratch_shapes=[pltpu.VMEM((kdim, tn), BF16)],
        compiler_params=_cparams(("arbitrary", "arbitrary")),
        name=name,
    )(*args)


def _norm_mod_body(x_ref, w_ref, sc_ref, sh_ref, *rest, with_router):
    x = x_ref[...]
    ms = jnp.mean(x * x, axis=-1, keepdims=True)
    y = x * lax.rsqrt(ms + EPS) * w_ref[...]
    h = y * (1.0 + sc_ref[...]) + sh_ref[...]
    if not with_router:
        (o_ref,) = rest
        o_ref[...] = h.astype(o_ref.dtype)
        return
    r_ref, o_ref, route_ref = rest
    half = h.shape[1] // 2
    hb = h.astype(BF16)
    lo = lax.bitcast_convert_type(hb[:, :half].astype(F32), jnp.uint32)
    hi = lax.bitcast_convert_type(hb[:, half:].astype(F32), jnp.uint32)
    o_ref[...] = lax.bitcast_convert_type((lo >> 16) | hi, F32)
    logits = jnp.dot(h, r_ref[...], preferred_element_type=F32, precision=HIGHEST)
    lane = lax.broadcasted_iota(jnp.int32, logits.shape, 1).astype(F32)
    neg = jnp.float32(-jnp.inf)
    lg = jnp.where(lane < N_EXPERTS, logits, neg)
    m1 = jnp.max(lg, axis=-1, keepdims=True)
    i1 = jnp.min(jnp.where(lg == m1, lane, float(LANES)), axis=-1, keepdims=True)
    lg2 = jnp.where(lane == i1, neg, lg)
    m2 = jnp.max(lg2, axis=-1, keepdims=True)
    i2 = jnp.min(jnp.where(lg2 == m2, lane, float(LANES)), axis=-1, keepdims=True)
    e = jnp.exp(m2 - m1)
    g1 = 1.0 / (1.0 + e)
    g2 = e / (1.0 + e)
    route = jnp.where(lane == 0, i1, jnp.where(lane == 1, i2, jnp.where(lane == 2, g1, jnp.where(lane == 3, g2, 0.0))))
    route_ref[...] = route


def _norm_mod(x, w_row, mod3, k_scale, k_shift, n_ctx_rows, router=None, tm=256):
    m, d = x.shape
    in_specs = [
        pl.BlockSpec((tm, d), lambda i: (i, 0)),
        pl.BlockSpec((1, d), lambda i: (0, 0)),
        pl.BlockSpec((None, 1, d), lambda i: (_mod_row(i, tm, n_ctx_rows), 0, k_scale)),
        pl.BlockSpec((None, 1, d), lambda i: (_mod_row(i, tm, n_ctx_rows), 0, k_shift)),
    ]
    args = [x, w_row, mod3, mod3]
    out_specs = pl.BlockSpec((tm, d), lambda i: (i, 0))
    out_shape = jax.ShapeDtypeStruct((m, d), BF16)
    if router is not None:
        in_specs.append(pl.BlockSpec((d, LANES), lambda i: (0, 0)))
        args.append(router)
        out_specs = [pl.BlockSpec((tm, d // 2), lambda i: (i, 0)), pl.BlockSpec((tm, LANES), lambda i: (i, 0))]
        out_shape = [jax.ShapeDtypeStruct((m, d // 2), F32), jax.ShapeDtypeStruct((m, LANES), F32)]
    return pl.pallas_call(
        functools.partial(_norm_mod_body, with_router=router is not None),
        grid=(m // tm,),
        in_specs=in_specs,
        out_specs=out_specs,
        out_shape=out_shape,
        compiler_params=_cparams(("arbitrary",)),
        name="norm_mod",
    )(*args)


def _resid_body(*refs, n_f, final):
    x_ref, f_refs, g_ref = refs[0], refs[1:1 + n_f], refs[1 + n_f]
    rest = refs[2 + n_f:]
    if n_f == 1:
        f = f_refs[0][...]
    else:
        route_ref, rest = rest[0], rest[1:]
        f = route_ref[:, 2:3] * f_refs[0][...] + route_ref[:, 3:4] * f_refs[1][...]
    y = x_ref[...] + g_ref[...] * f
    if final:
        w_ref, o_ref = rest
        ms = jnp.mean(y * y, axis=-1, keepdims=True)
        y = y * lax.rsqrt(ms + EPS) * w_ref[...]
    else:
        (o_ref,) = rest
    o_ref[...] = y


def _resid(x, fs, mod3, k_gate, n_ctx_rows, route=None, final_w=None, tm=256):
    m, d = x.shape
    tok = pl.BlockSpec((tm, d), lambda i: (i, 0))
    in_specs = [tok] + [tok] * len(fs) + [
        pl.BlockSpec((None, 1, d), lambda i: (_mod_row(i, tm, n_ctx_rows), 0, k_gate))]
    args = [x, *fs, mod3]
    if route is not None:
        in_specs.append(pl.BlockSpec((tm, LANES), lambda i: (i, 0)))
        args.append(route)
    if final_w is not None:
        in_specs.append(pl.BlockSpec((1, d), lambda i: (0, 0)))
        args.append(final_w)
    return pl.pallas_call(
        functools.partial(_resid_body, n_f=len(fs), final=final_w is not None),
        grid=(m // tm,),
        in_specs=in_specs,
        out_specs=pl.BlockSpec((tm, d), lambda i: (i, 0)),
        out_shape=jax.ShapeDtypeStruct((m, d), F32),
        compiler_params=_cparams(("arbitrary",)),
        name="resid",
    )(*args)


FFN_ROWS = 1024
FFN_TF = 256
FFN_KC = 1024
FFN_NC = 512
FFN_VMEM_LIMIT = 62 * 1024 * 1024
FFN_DENSE_ROWS = (FFN_ROWS,)
FFN_MOE_ROWS = (256, 512, 768, FFN_ROWS)


def _ffn_body(ie_ref, ib_ref, nv_ref, x_ref, wg_ref, wu_ref, wd_ref, o_ref, *, packed, row_variants):
    it = pl.program_id(0)
    f = pl.program_id(1)
    nv = nv_ref[it]
    d = wg_ref.shape[0]
    gran = row_variants[0]

    @pl.when(jnp.logical_and(f == 0, nv > 0))
    def _():
        o_ref[...] = jnp.zeros_like(o_ref)

    def x_chunk(m, kc):
        if not packed:
            return x_ref[0:m, kc * FFN_KC:(kc + 1) * FFN_KC]
        per_half = (d // 2) // FFN_KC
        c0 = (kc % per_half) * FFN_KC
        w = lax.bitcast_convert_type(x_ref[0:m, c0:c0 + FFN_KC], jnp.uint32)
        bits = (w << 16) if kc < per_half else (w & jnp.uint32(0xFFFF0000))
        return lax.bitcast_convert_type(bits, F32).astype(BF16)

    def compute(m):
        g = u = None
        for kc in range(d // FFN_KC):
            ks = slice(kc * FFN_KC, (kc + 1) * FFN_KC)
            xk = x_chunk(m, kc)
            pg = jnp.dot(xk, wg_ref[ks, :].astype(BF16), preferred_element_type=F32)
            pu = jnp.dot(xk, wu_ref[ks, :].astype(BF16), preferred_element_type=F32)
            g = pg if g is None else g + pg
            u = pu if u is None else u + pu
        a = (jax.nn.silu(g) * u).astype(BF16)
        for nc in range(d // FFN_NC):
            ns = slice(nc * FFN_NC, (nc + 1) * FFN_NC)
            o_ref[0:m, ns] += jnp.dot(a, wd_ref[:, ns].astype(BF16), preferred_element_type=F32)

    n_gran = (nv + gran - 1) // gran
    for m in row_variants:
        @pl.when(n_gran == m // gran)
        def _(m=m):
            compute(m)


def _ffn(xs, wg, wu, wd, lead, item_expert, item_block, item_rows, *, packed, row_variants):
    r = xs.shape[0]
    d = wg.shape[-2]
    n_items = item_expert.shape[0]
    dff = wg.shape[-1]
    n_f = dff // FFN_TF
    assert all(m % row_variants[0] == 0 for m in row_variants) and row_variants[-1] == FFN_ROWS

    def f_idx(it, f, nv):
        return jnp.where(nv[it] > 0, f, n_f - 1)

    if wg.ndim == 4:
        wgu_spec = pl.BlockSpec((None, None, d, FFN_TF), lambda it, f, ie, ib, nv: (lead, ie[it], 0, f_idx(it, f, nv)))
        wd_spec = pl.BlockSpec((None, None, FFN_TF, d), lambda it, f, ie, ib, nv: (lead, ie[it], f_idx(it, f, nv), 0))
    else:
        wgu_spec = pl.BlockSpec((None, d, FFN_TF), lambda it, f, ie, ib, nv: (lead, 0, f_idx(it, f, nv)))
        wd_spec = pl.BlockSpec((None, FFN_TF, d), lambda it, f, ie, ib, nv: (lead, f_idx(it, f, nv), 0))
    grid_spec = pltpu.PrefetchScalarGridSpec(
        num_scalar_prefetch=3,
        grid=(n_items, n_f),
        in_specs=[
            pl.BlockSpec((FFN_ROWS, xs.shape[1]), lambda it, f, ie, ib, nv: (ib[it], 0),
                         pipeline_mode=pl.Buffered(1)),
            wgu_spec, wgu_spec, wd_spec,
        ],
        out_specs=pl.BlockSpec((FFN_ROWS, d), lambda it, f, ie, ib, nv: (ib[it], 0), pipeline_mode=pl.Buffered(1)),
    )
    return pl.pallas_call(
        functools.partial(_ffn_body, packed=packed, row_variants=row_variants),
        grid_spec=grid_spec,
        out_shape=jax.ShapeDtypeStruct((r, d), F32),
        compiler_params=_cparams(("arbitrary", "arbitrary"), vmem=FFN_VMEM_LIMIT),
        name="ffn",
    )(item_expert, item_block, item_rows, xs, wg, wu, wd)


def _moe_dispatch(route, n_tokens):
    e_idx = route[:, :2].astype(jnp.int32)
    flat_e = e_idx.reshape(-1)
    n_pairs = flat_e.shape[0]
    onehot = (flat_e[:, None] == jnp.arange(N_EXPERTS)[None, :]).astype(jnp.int32)
    seen = jnp.cumsum(onehot, axis=0)
    counts = seen[-1]
    rank = jnp.sum(onehot * seen, axis=1) - 1
    blocks_per = (counts + FFN_ROWS - 1) // FFN_ROWS
    blk_end = jnp.cumsum(blocks_per)
    blk_start = blk_end - blocks_per
    dest = jnp.sum(onehot * blk_start[None, :], axis=1) * FFN_ROWS + rank
    n_items = (n_pairs + N_EXPERTS * (FFN_ROWS - 1)) // FFN_ROWS
    n_rows = n_items * FFN_ROWS
    token_of_row = jnp.zeros((n_rows,), jnp.int32).at[dest].set(jnp.arange(n_pairs, dtype=jnp.int32) // 2)
    row_of_pair = dest.reshape(n_tokens, 2)
    blk = jnp.arange(n_items, dtype=jnp.int32)
    n_real = blk_end[-1]
    blk_c = jnp.minimum(blk, n_real - 1)
    item_e = jnp.minimum(jnp.sum(blk_c[:, None] >= blk_end[None, :], axis=1), N_EXPERTS - 1).astype(jnp.int32)
    rows_left = counts[item_e] - (blk_c - blk_start[item_e]) * FFN_ROWS
    item_rows = jnp.where(blk < n_real, jnp.clip(rows_left, 0, FFN_ROWS), 0).astype(jnp.int32)
    return token_of_row, row_of_pair, item_e, blk_c.astype(jnp.int32), item_rows


def _s5_prep(a_re, a_im, log_dt, b_re, b_im, c_re, c_im, d):
    q, g, p, h = S5_Q, S5_GROUPS, S5_P, S5_H
    dt = jnp.exp(log_dt)[..., None]
    k = jnp.arange(q + 1, dtype=F32)
    mag = jnp.exp((a_re * dt)[..., None] * k)
    ang = (a_im * dt)[..., None] * k
    pw_re, pw_im = mag * jnp.cos(ang), mag * jnp.sin(ang)
    ab_re, ab_im = pw_re[..., 1], pw_im[..., 1]
    den = a_re * a_re + a_im * a_im
    q_re = ((ab_re - 1.0) * a_re + ab_im * a_im) / den
    q_im = (ab_im * a_re - (ab_re - 1.0) * a_im) / den
    bb_re = q_re[..., None] * b_re[None] - q_im[..., None] * b_im[None]
    bb_im = q_re[..., None] * b_im[None] + q_im[..., None] * b_re[None]
    c_re_t = jnp.swapaxes(c_re, 1, 2)[None, ..., None]
    c_im_t = jnp.swapaxes(c_im, 1, 2)[None, ..., None]
    cp_re = c_re_t * pw_re[:, :, :, None, :] - c_im_t * pw_im[:, :, :, None, :]
    cp_im = c_re_t * pw_im[:, :, :, None, :] + c_im_t * pw_re[:, :, :, None, :]
    cpx_re = jnp.swapaxes(cp_re, 3, 4)[..., None]
    cpx_im = jnp.swapaxes(cp_im, 3, 4)[..., None]
    bbx_re = bb_re[:, :, :, None, None, :]
    bbx_im = bb_im[:, :, :, None, None, :]
    kern = jnp.transpose(jnp.sum(cpx_re * bbx_re - cpx_im * bbx_im, axis=2), (0, 2, 1, 3, 4))
    ii = jnp.arange(q)[:, None]
    jj = jnp.arange(q)[None, :]
    lag = ii - jj
    kf = jnp.where((lag >= 0)[..., None, None, None], kern[0][jnp.clip(lag, 0, q)], 0.0)
    kb = jnp.where((lag <= 0)[..., None, None, None], kern[1][jnp.clip(-lag, 0, q)], 0.0)
    m_intra = jnp.transpose(kf + kb, (2, 1, 4, 0, 3)).reshape(g, q * h, q * h)
    pf_re, pf_im = pw_re[0][..., q - 1 - jnp.arange(q)], pw_im[0][..., q - 1 - jnp.arange(q)]
    pb_re, pb_im = pw_re[1][..., :q], pw_im[1][..., :q]

    def contrib(p_re, p_im, b_r, b_i):
        w_re = p_re[:, :, :, None] * b_r[:, :, None, :] - p_im[:, :, :, None] * b_i[:, :, None, :]
        w_im = p_re[:, :, :, None] * b_i[:, :, None, :] + p_im[:, :, :, None] * b_r[:, :, None, :]
        to_rows = lambda w: jnp.transpose(w, (0, 2, 3, 1)).reshape(g, q * h, p)
        return to_rows(w_re), to_rows(w_im)

    wsf_re, wsf_im = contrib(pf_re, pf_im, bb_re[0], bb_im[0])
    wsb_re, wsb_im = contrib(pb_re, pb_im, bb_re[1], bb_im[1])
    w_state = jnp.concatenate([wsf_re, wsb_re, wsf_im, wsb_im], axis=-1)
    to_cols = lambda w: jnp.transpose(w, (0, 1, 3, 2)).reshape(g, p, q * h)
    fi = jnp.arange(q) + 1
    bi = q - jnp.arange(q)
    w_off = jnp.concatenate([to_cols(cp_re[0][..., fi]), to_cols(cp_re[1][..., bi]),
                             to_cols(-cp_im[0][..., fi]), to_cols(-cp_im[1][..., bi])], axis=1)
    aq_re, aq_im = pw_re[..., q], pw_im[..., q]
    a1 = jnp.concatenate([aq_re[0], aq_re[1]], axis=-1)[:, None, :]
    a2 = jnp.concatenate([aq_im[0], aq_im[1]], axis=-1)[:, None, :]
    d_row = jnp.tile(d.reshape(g, 1, h), (1, 1, q))
    return m_intra.astype(BF16), w_state.astype(BF16), w_off.astype(BF16), a1, a2, d_row


def _s5_body(u_ref, mi_ref, ws_ref, wo_ref, a1_ref, a2_ref, d_ref, x0_ref, y_ref, fin_ref, con_ref, sp_ref,
             *, n_ctx_b, n_ctx_c, n_lat_b, n_lat_c):
    half = 2 * S5_P
    ub = u_ref[...]
    u = ub.astype(F32)
    con_ref[...] = jnp.dot(ub, ws_ref[...], preferred_element_type=F32)

    def scan(row0, nb, nc, s_re, s_im):
        ar = jnp.broadcast_to(a1_ref[...], (nb, half))
        ai = jnp.broadcast_to(a2_ref[...], (nb, half))
        is_fwd = lax.broadcasted_iota(jnp.int32, (nb, half), 1) < S5_P
        for k in range(nc):
            rf = slice(row0 + k * nb, row0 + (k + 1) * nb)
            rb = slice(row0 + (nc - 1 - k) * nb, row0 + (nc - k) * nb)
            sp_ref[rf, 0:S5_P] = s_re[:, :S5_P]
            sp_ref[rb, S5_P:half] = s_re[:, S5_P:]
            sp_ref[rf, half:half + S5_P] = s_im[:, :S5_P]
            sp_ref[rb, half + S5_P:2 * half] = s_im[:, S5_P:]
            c_re = jnp.where(is_fwd, con_ref[rf, 0:half], con_ref[rb, 0:half])
            c_im = jnp.where(is_fwd, con_ref[rf, half:2 * half], con_ref[rb, half:2 * half])
            s_re, s_im = ar * s_re - ai * s_im + c_re, ar * s_im + ai * s_re + c_im
        return s_re, s_im

    zeros = jnp.zeros((n_ctx_b, half), F32)
    fin_ref[:, 0:half], fin_ref[:, half:2 * half] = scan(0, n_ctx_b, n_ctx_c, zeros, zeros)
    scan(n_ctx_b * n_ctx_c, n_lat_b, n_lat_c, x0_ref[:, 0:half], x0_ref[:, half:2 * half])
    y = jnp.dot(ub, mi_ref[...], preferred_element_type=F32)
    y = y + jnp.dot(sp_ref[...].astype(BF16), wo_ref[...], preferred_element_type=F32)
    y_ref[...] = (y + u * d_ref[...]).astype(y_ref.dtype)


def _s5_scan(u_rows, ops, x0, n_ctx_b, n_ctx_c, n_lat_b, n_lat_c):
    g, r, w = u_rows.shape
    m_intra, w_state, w_off, a1, a2, d_row = ops
    mat = pl.BlockSpec((None, w, w), lambda i: (i, 0, 0))
    row = pl.BlockSpec((None, 1, w), lambda i: (i, 0, 0))
    return pl.pallas_call(
        functools.partial(_s5_body, n_ctx_b=n_ctx_b, n_ctx_c=n_ctx_c, n_lat_b=n_lat_b, n_lat_c=n_lat_c),
        grid=(g,),
        in_specs=[pl.BlockSpec((None, r, w), lambda i: (i, 0, 0)), mat, mat, mat,
                  pl.BlockSpec((None, 1, w // 2), lambda i: (i, 0, 0)),
                  pl.BlockSpec((None, 1, w // 2), lambda i: (i, 0, 0)), row,
                  pl.BlockSpec((None, n_lat_b, w), lambda i: (i, 0, 0))],
        out_specs=[pl.BlockSpec((None, r, w), lambda i: (i, 0, 0)),
                   pl.BlockSpec((None, n_ctx_b, w), lambda i: (i, 0, 0))],
        out_shape=[jax.ShapeDtypeStruct((g, r, w), BF16), jax.ShapeDtypeStruct((g, n_ctx_b, w), F32)],
        scratch_shapes=[pltpu.VMEM((r, w), F32), pltpu.VMEM((r, w), F32)],
        compiler_params=_cparams(("arbitrary",)),
        name="s5_scan",
    )(u_rows, m_intra, w_state, w_off, a1, a2, d_row, x0)


def _s5_glu_body(y_ref, w_ref, o_ref, wbf_ref):
    @pl.when(pl.program_id(0) == 0)
    def _():
        wbf_ref[...] = w_ref[...].astype(BF16)

    y = jax.nn.gelu(y_ref[...].astype(F32))
    z = jnp.dot(y.astype(BF16), wbf_ref[...], preferred_element_type=F32)
    o_ref[...] = (y * jax.nn.sigmoid(z)).astype(o_ref.dtype)


def _s5_glu(y, w_glu, lead, tm=512):
    m, d = y.shape
    return pl.pallas_call(
        _s5_glu_body,
        grid=(m // tm,),
        in_specs=[pl.BlockSpec((tm, d), lambda i: (i, 0)), pl.BlockSpec((None, d, d), lambda i: (lead, 0, 0))],
        out_specs=pl.BlockSpec((tm, d), lambda i: (i, 0)),
        out_shape=jax.ShapeDtypeStruct((m, d), BF16),
        scratch_shapes=[pltpu.VMEM((d, d), BF16)],
        compiler_params=_cparams(("arbitrary",)),
        name="s5_glu",
    )(y, w_glu)


def _conv3(u, w_ref, b_ref, seg):
    n = u.shape[0]
    assert seg & (seg - 1) == 0
    pos = lax.broadcasted_iota(jnp.int32, u.shape, 0) & (seg - 1)
    prev = jnp.where(pos == 0, 0.0, pltpu.roll(u, 1, 0))
    nxt = jnp.where(pos == seg - 1, 0.0, pltpu.roll(u, n - 1, 0))
    return b_ref[...] + w_ref[0:1, :] * prev + w_ref[1:2, :] * u + w_ref[2:3, :] * nxt


def _softplus(x):
    return jnp.maximum(x, 0.0) + jnp.log(1.0 + jnp.exp(-jnp.abs(x)))


def _ssd_body(*refs, seq, seg, has_init, want_fin, n_aliased):
    (x_ref, b_ref, c_ref, dt_ref, cwx_ref, cwb_ref, cwc_ref, cbx_ref, cbb_ref, cbc_ref,
     dtb_ref, arow_ref, drow_ref) = refs[:13]
    k = 13
    init_ref = None
    if has_init:
        init_ref = refs[k]
        k += 1
    k += n_aliased
    y_ref = refs[k]
    k += 1
    fin_ref = None
    if want_fin:
        fin_ref = refs[k]
        k += 1
    xc_ref, bc_ref, cc_ref, stf_ref, stb_ref, xsb_ref, rcsx_ref = refs[k:k + 7]
    q, p, hpg = SSD_CHUNK, SSD_HEADDIM, SSD_HPG
    wid = hpg * p
    nc = seq // q

    xc_ref[...] = jax.nn.silu(_conv3(x_ref[...], cwx_ref, cbx_ref, seg))
    bc_ref[...] = jax.nn.silu(_conv3(b_ref[...], cwb_ref, cbb_ref, seg))
    cc_ref[...] = jax.nn.silu(_conv3(c_ref[...], cwc_ref, cbc_ref, seg))
    if has_init:
        stf_ref[...] = jnp.concatenate([init_ref[0, r] for r in range(hpg)], axis=0).T
        stb_ref[...] = jnp.concatenate([init_ref[1, r] for r in range(hpg)], axis=0).T
    else:
        stf_ref[...] = jnp.zeros_like(stf_ref)
        stb_ref[...] = jnp.zeros_like(stb_ref)

    ii = lax.broadcasted_iota(jnp.int32, (q, q), 0)
    jj = lax.broadcasted_iota(jnp.int32, (q, q), 1)
    tri_lo = (ii >= jj).astype(BF16)
    er = lax.broadcasted_iota(jnp.int32, (LANES, 2 * wid), 0)
    ec = lax.broadcasted_iota(jnp.int32, (LANES, 2 * wid), 1) // p
    expand = (er == ec).astype(BF16)
    neg = jnp.float32(-jnp.inf)
    bdot = functools.partial(jnp.dot, preferred_element_type=F32)

    def split3(v):
        v1 = v.astype(BF16)
        r1 = v - v1.astype(F32)
        v2 = r1.astype(BF16)
        return v1, v2, (r1 - v2.astype(F32)).astype(BF16)

    def cumsum_rows(v):
        r = bdot(tri_lo, jnp.concatenate(split3(v), axis=1))
        n = v.shape[1]
        return r[:, :n] + r[:, n:2 * n] + r[:, 2 * n:]

    def expand_heads(vs):
        r = bdot(jnp.concatenate([s for v in vs for s in split3(v)], axis=0), expand)
        return [r[3 * i * q:(3 * i + 1) * q] + r[(3 * i + 1) * q:(3 * i + 2) * q] + r[(3 * i + 2) * q:(3 * i + 3) * q]
                for i in range(len(vs))]

    def fwd(c, carry):
        r0 = pl.multiple_of(c * q, q)
        dt = _softplus(dt_ref[pl.ds(r0, q), :] + dtb_ref[...])
        adt = dt * arow_ref[...]
        cs = cumsum_rows(adt)
        rcs = cs[q - 1:q, :] - cs + adt
        lane = lax.broadcasted_iota(jnp.int32, (q, LANES), 1)
        cs_t = jnp.where(lane < hpg, cs, rcs).T
        x = xc_ref[pl.ds(r0, q), :]
        bm = bc_ref[pl.ds(r0, q), :]
        cm = cc_ref[pl.ds(r0, q), :].astype(BF16)
        cb = lax.dot_general(cm, bm.astype(BF16), (((1,), (1,)), ((), ())), preferred_element_type=F32)
        dtx, csx, rcsx = expand_heads([dt, cs, rcs])
        csx, rcsx = csx[:, :wid], rcsx[:, wid:]
        xs_f = x * dtx[:, :wid]
        xs_b = x * dtx[:, wid:]
        xsb_ref[pl.ds(r0, q), :] = xs_b
        rcsx_ref[pl.ds(r0, q), :] = rcsx
        parts = []
        for r in range(hpg):
            cols = slice(r * p, (r + 1) * p)
            lf = jnp.exp(jnp.where(ii >= jj, cs[:, r:r + 1] - cs_t[r:r + 1, :], neg))
            lb = jnp.exp(jnp.where(jj >= ii, rcs[:, hpg + r:hpg + r + 1] - cs_t[hpg + r:hpg + r + 1, :], neg))
            lhs = jnp.concatenate([cb * lf, cb * lb], axis=1).astype(BF16)
            rhs = jnp.concatenate([xs_f[:, cols], xs_b[:, cols]], axis=0).astype(BF16)
            parts.append(bdot(lhs, rhs))
        y = jnp.concatenate(parts, axis=1)
        st = stf_ref[...]
        y = y + bdot(cm, st.astype(BF16)) * jnp.exp(csx) + drow_ref[...] * x
        y_ref[pl.ds(r0, q), :] = y
        tot = csx[q - 1:q, :]
        dec = jnp.exp(tot - csx)
        stf_ref[...] = st * jnp.exp(tot) + bdot(bm.T.astype(BF16), (xs_f * dec).astype(BF16))
        return carry

    def bwd(k_it, carry):
        c = nc - 1 - k_it
        r0 = pl.multiple_of(c * q, q)
        rcsx = rcsx_ref[pl.ds(r0, q), :]
        xs_b = xsb_ref[pl.ds(r0, q), :]
        bm = bc_ref[pl.ds(r0, q), :]
        cm = cc_ref[pl.ds(r0, q), :].astype(BF16)
        st = stb_ref[...]
        y_ref[pl.ds(r0, q), :] += bdot(cm, st.astype(BF16)) * jnp.exp(rcsx)
        tot = rcsx[0:1, :]
        dec = jnp.exp(tot - rcsx)
        stb_ref[...] = st * jnp.exp(tot) + bdot(bm.T.astype(BF16), (xs_b * dec).astype(BF16))
        return carry

    lax.fori_loop(0, nc, fwd, 0)
    lax.fori_loop(0, nc, bwd, 0)
    if want_fin:
        fin_f, fin_b = stf_ref[...].T, stb_ref[...].T
        for r in range(hpg):
            fin_ref[0, r] = fin_f[r * p:(r + 1) * p, :]
            fin_ref[1, r] = fin_b[r * p:(r + 1) * p, :]


def _ssd_scan(proj, dtg, conv_w, conv_b, dtb_rows, a_rows, d_rows, init, y_prev, fin_prev, *, layer, row0, n_b,
              seq, seg, want_fin):
    q, p, hpg, ng, ns = SSD_CHUNK, SSD_HEADDIM, SSD_HPG, SSD_NGROUPS, SSD_DSTATE
    wid = hpg * p
    rb0 = row0 // seq
    xcol0 = OFF_XBC // wid
    bcol0 = (OFF_XBC + D_SSD) // ns
    ccol0 = bcol0 + ng
    in_specs = [
        pl.BlockSpec((seq, wid), lambda b, g: (rb0 + b, xcol0 + g)),
        pl.BlockSpec((seq, ns), lambda b, g: (rb0 + b, bcol0 + g)),
        pl.BlockSpec((seq, ns), lambda b, g: (rb0 + b, ccol0 + g)),
        pl.BlockSpec((None, seq, LANES), lambda b, g: (g, rb0 + b, 0)),
        pl.BlockSpec((3, wid), lambda b, g: (0, g)),
        pl.BlockSpec((3, ns), lambda b, g: (0, D_SSD // ns + g)),
        pl.BlockSpec((3, ns), lambda b, g: (0, D_SSD // ns + ng + g)),
        pl.BlockSpec((1, wid), lambda b, g: (0, g)),
        pl.BlockSpec((1, ns), lambda b, g: (0, D_SSD // ns + g)),
        pl.BlockSpec((1, ns), lambda b, g: (0, D_SSD // ns + ng + g)),
        pl.BlockSpec((None, 1, LANES), lambda b, g: (g, 0, 0)),
        pl.BlockSpec((None, 1, LANES), lambda b, g: (g, 0, 0)),
        pl.BlockSpec((None, 1, wid), lambda b, g: (g, 0, 0)),
    ]
    args = [proj, proj, proj, dtg, conv_w, conv_w, conv_w, conv_b, conv_b, conv_b, dtb_rows, a_rows, d_rows]
    if init is not None:
        in_specs.append(pl.BlockSpec((None, 2, hpg, p, ns), lambda b, g: (b, 0, g, 0, 0)))
        args.append(init)
    aliases = {}
    for k_out, prev in enumerate((y_prev, fin_prev)):
        if prev is not None:
            in_specs.append(pl.BlockSpec(memory_space=pl.ANY))
            args.append(prev)
            aliases[len(args) - 1] = k_out
    out_specs = [pl.BlockSpec((seq, wid), lambda b, g: (rb0 + b, g))]
    out_shape = [jax.ShapeDtypeStruct((proj.shape[0], D_SSD), F32)]
    if want_fin:
        out_specs.append(pl.BlockSpec((None, None, 2, hpg, p, ns), lambda b, g: (b, layer, 0, g, 0, 0)))
        out_shape.append(jax.ShapeDtypeStruct((n_b, DEPTH, 2, SSD_HEADS, p, ns), F32))
    else:
        assert fin_prev is None
    res = pl.pallas_call(
        functools.partial(_ssd_body, seq=seq, seg=seg, has_init=init is not None, want_fin=want_fin,
                          n_aliased=len(aliases)),
        grid=(n_b, ng),
        in_specs=in_specs,
        out_specs=out_specs,
        out_shape=out_shape,
        input_output_aliases=aliases,
        scratch_shapes=[pltpu.VMEM((seq, wid), F32), pltpu.VMEM((seq, ns), F32), pltpu.VMEM((seq, ns), F32),
                        pltpu.VMEM((ns, wid), F32), pltpu.VMEM((ns, wid), F32),
                        pltpu.VMEM((seq, wid), F32), pltpu.VMEM((seq, wid), F32)],
        compiler_params=_cparams(("arbitrary", "arbitrary")),
        name="ssd_scan",
    )(*args)
    return res if want_fin else (res[0], None)


def _ssd_norm_body(y_ref, z0_ref, z1_ref, w_ref, o_ref):
    z = jnp.concatenate([z0_ref[...], z1_ref[...]], axis=1)
    y = y_ref[...] * jax.nn.silu(z)
    ms = jnp.mean(y * y, axis=-1, keepdims=True)
    o_ref[...] = (y * lax.rsqrt(ms + EPS) * w_ref[...]).astype(o_ref.dtype)


def _ssd_norm(y, proj, norm_w, tm=256):
    m, d = y.shape
    half = d // 2
    zb = OFF_Z // half
    return pl.pallas_call(
        _ssd_norm_body,
        grid=(m // tm,),
        in_specs=[pl.BlockSpec((tm, d), lambda i: (i, 0)),
                  pl.BlockSpec((tm, half), lambda i: (i, zb)),
                  pl.BlockSpec((tm, half), lambda i: (i, zb + 1)),
                  pl.BlockSpec((1, d), lambda i: (0, 0))],
        out_specs=pl.BlockSpec((tm, d), lambda i: (i, 0)),
        out_shape=jax.ShapeDtypeStruct((m, d), BF16),
        compiler_params=_cparams(("arbitrary",)),
        name="ssd_norm",
    )(y, proj, proj, norm_w)


def _hyena_filters(seq, w1, b1, w2, b2, w3, freq):
    t = jnp.arange(seq, dtype=F32) / seq
    bands = jnp.linspace(1e-4, HY_BANDS - 1, HY_BANDS, dtype=F32)
    ang = 2.0 * math.pi * t[:, None] * bands[None, :]
    feat = jnp.concatenate([t[:, None], jnp.cos(ang), -jnp.sin(ang)], axis=-1)
    dot = functools.partial(jnp.dot, precision=HIGHEST)
    hdn = jnp.sin(freq[0] * (dot(feat, w1) + b1))
    hdn = jnp.sin(freq[1] * (dot(hdn, w2) + b2))
    filt = dot(hdn, w3).reshape(seq, 2, HY_ORDER, D_HY)
    deltas = jnp.abs(jnp.linspace(HY_MIN_DECAY, HY_MAX_DECAY, D_HY, dtype=F32))
    filt = filt * jnp.exp(-t[:, None] * deltas[None, :])[:, None, None, :]
    hf = filt[:, 0].reshape(seq, HY_ORDER * D_HY)
    hb = filt[:, 1].reshape(seq, HY_ORDER * D_HY)
    return hf + hb, hf - hb


def _split_bf16(x):
    hi = x.astype(BF16)
    return hi, (x - hi.astype(F32)).astype(BF16)


def _dft_ops(seq):
    n = 2 * seq
    f = jnp.arange(seq, dtype=jnp.int32)[:, None]
    t = jnp.arange(seq, dtype=jnp.int32)[None, :]
    ang = ((f * t) % n).astype(F32) * (math.pi / seq)
    cos_ft, sin_ft = jnp.cos(ang), jnp.sin(ang)
    nyq = jnp.where(t % 2 == 0, 1.0, -1.0).astype(F32)
    fwd = jnp.concatenate([cos_ft, nyq, -sin_ft[1:]], axis=0)
    wgt = jnp.where(f == 0, 1.0, 2.0).astype(F32) / n
    inv = jnp.concatenate([(wgt * cos_ft).T, nyq.T / n, (-(2.0 / n) * sin_ft[1:]).T], axis=1)
    fh, fl = _split_bf16(fwd)
    gh, gl = _split_bf16(inv)
    return jnp.concatenate([fh, fh, fl], axis=1), jnp.concatenate([gh, gh, gl], axis=1)


def _dot3(op_cat, x):
    x_hi, x_lo = _split_bf16(x)
    return jnp.dot(op_cat, jnp.concatenate([x_hi, x_lo, x_hi], axis=0), preferred_element_type=F32)


def _spectra_body(hs_ref, hd_ref, f_ref, kr_ref, ki_ref, kn_ref, *, seq):
    fs = _dot3(f_ref[...], hs_ref[...])
    fd = _dot3(f_ref[...], hd_ref[...])
    row0 = lax.broadcasted_iota(jnp.int32, (seq, 1), 0) == 0
    kr_ref[...] = fs[:seq]
    kn_ref[...] = jnp.broadcast_to(fs[seq:seq + 1], kn_ref.shape)
    ki_ref[...] = jnp.where(row0, 0.0, fd[seq:])


def _hyena_spectra(hsum, hdiff, dft, seq, tc=256):
    n = hsum.shape[1]
    blk = pl.BlockSpec((seq, tc), lambda cb: (0, cb))
    const = pl.BlockSpec((2 * seq, 3 * seq), lambda cb: (0, 0), pipeline_mode=pl.Buffered(1))
    return pl.pallas_call(
        functools.partial(_spectra_body, seq=seq),
        grid=(n // tc,),
        in_specs=[blk, blk, const],
        out_specs=[blk, blk, pl.BlockSpec((8, tc), lambda cb: (0, cb))],
        out_shape=[jax.ShapeDtypeStruct((seq, n), F32)] * 2 + [jax.ShapeDtypeStruct((8, n), F32)],
        compiler_params=_cparams(("arbitrary",)),
        name="hyena_spectra",
    )(hsum, hdiff, dft[0])


HY_VMEM_LIMIT = 61 * 1024 * 1024


def _hyena_body(*refs, seq, seg, aliased):
    (v_ref, x1_ref, x2_ref, cwv_ref, cw1_ref, cw2_ref, cbv_ref, cb1_ref, cb2_ref,
     kr0_ref, kr1_ref, ki0_ref, ki1_ref, kn0_ref, kn1_ref, bias_ref, f_ref, g_ref) = refs[:18]
    o_ref = refs[-1]
    kr, ki, kn = (kr0_ref, kr1_ref), (ki0_ref, ki1_ref), (kn0_ref, kn1_ref)
    z = _conv3(v_ref[...], cwv_ref, cbv_ref, seg)
    gates = (_conv3(x1_ref[...], cw1_ref, cb1_ref, seg), _conv3(x2_ref[...], cw2_ref, cb2_ref, seg))
    row0 = lax.broadcasted_iota(jnp.int32, (seq, 1), 0) == 0
    for o in range(HY_ORDER):
        zf = _dot3(f_ref[...], z)
        a, b = zf[:seq], zf[seq:]
        pre = a * kr[o][...] - b * ki[o][...]
        pim = a * ki[o][...] + b * jnp.where(row0, kn[o][0:1, :], kr[o][...])
        conv = _dot3(g_ref[...], jnp.concatenate([pre, pim], axis=0))
        z = gates[o] * (conv + bias_ref[o:o + 1, :] * z)
    o_ref[...] = z.astype(o_ref.dtype)


def _hyena(proj, conv_w, conv_b, spectra, bias, dft, out_prev, *, n_rows, row0, n_b, seq, seg, tc):
    assert HY_ORDER == 2
    kr, ki, kn = spectra
    f_cat, g_cat = dft
    rb0 = row0 // seq
    ncb = D_HY // tc
    assert PROJ_HY % tc == 0
    u_spec = lambda part: pl.BlockSpec((seq, tc), lambda cb, b: (rb0 + b, PROJ_HY // tc + part * ncb + cb),
                                       pipeline_mode=pl.Buffered(1))
    w_spec = lambda part: pl.BlockSpec((3, tc), lambda cb, b: (0, part * ncb + cb))
    b_spec = lambda part: pl.BlockSpec((1, tc), lambda cb, b: (0, part * ncb + cb))
    k_spec = lambda o: pl.BlockSpec((seq, tc), lambda cb, b: (0, o * ncb + cb), pipeline_mode=pl.Buffered(1))
    n_spec = lambda o: pl.BlockSpec((8, tc), lambda cb, b: (0, o * ncb + cb))
    const = lambda shape: pl.BlockSpec(shape, lambda cb, b: (0, 0), pipeline_mode=pl.Buffered(1))
    in_specs = [u_spec(0), u_spec(1), u_spec(2), w_spec(0), w_spec(1), w_spec(2), b_spec(0), b_spec(1), b_spec(2),
                k_spec(0), k_spec(1), k_spec(0), k_spec(1), n_spec(0), n_spec(1),
                pl.BlockSpec((HY_ORDER, tc), lambda cb, b: (0, cb)),
                const((2 * seq, 3 * seq)), const((seq, 6 * seq))]
    args = [proj, proj, proj, conv_w, conv_w, conv_w, conv_b, conv_b, conv_b, kr, kr, ki, ki, kn, kn, bias,
            f_cat, g_cat]
    aliases = {}
    if out_prev is not None:
        in_specs.append(pl.BlockSpec(memory_space=pl.ANY))
        args.append(out_prev)
        aliases = {len(args) - 1: 0}
    return pl.pallas_call(
        functools.partial(_hyena_body, seq=seq, seg=seg, aliased=out_prev is not None),
        grid=(ncb, n_b),
        in_specs=in_specs,
        out_specs=pl.BlockSpec((seq, tc), lambda cb, b: (rb0 + b, cb)),
        out_shape=jax.ShapeDtypeStruct((n_rows, D_HY), BF16),
        input_output_aliases=aliases,
        compiler_params=_cparams(("arbitrary", "arbitrary"), vmem=HY_VMEM_LIMIT),
        name="hyena",
    )(*args)


def _s5_mixer(proj, p, ops, l, dims):
    n_cb, n_cs, n_lb, n_ls = dims
    t_ctx = n_cb * n_cs
    nq_c, nq_l = n_cs // S5_Q, n_ls // S5_Q
    u = lax.optimization_barrier(proj[:, :D_S5]).astype(BF16)

    def to_rows(a, nb, nq):
        a = jnp.transpose(a.reshape(nb, nq, S5_Q, S5_GROUPS, S5_H), (3, 1, 0, 2, 4))
        return a.reshape(S5_GROUPS, nq * nb, S5_Q * S5_H)

    def from_rows(a, nb, nq):
        a = jnp.transpose(a.reshape(S5_GROUPS, nq, nb, S5_Q, S5_H), (2, 1, 3, 0, 4))
        return a.reshape(nb * nq * S5_Q, D_S5)

    u_rows = jnp.concatenate([to_rows(u[:t_ctx], n_cb, nq_c), to_rows(u[t_ctx:], n_lb, nq_l)], axis=1)
    sre, sim = p['state_s5_re'][:, l], p['state_s5_im'][:, l]
    x0 = jnp.transpose(jnp.concatenate([sre[:, 0], sre[:, 1], sim[:, 0], sim[:, 1]], axis=-1), (1, 0, 2))
    y_rows, fin = _s5_scan(u_rows, ops, x0, n_cb, nq_c, n_lb, nq_l)
    y = jnp.concatenate([from_rows(y_rows[:, :nq_c * n_cb], n_cb, nq_c),
                         from_rows(y_rows[:, nq_c * n_cb:], n_lb, nq_l)], axis=0)
    y = _s5_glu(y, p['s5_w_glu'], l)
    fin = jnp.transpose(fin.reshape(S5_GROUPS, n_cb, 4, S5_P), (1, 2, 0, 3))
    return y, fin[:, 0:2], fin[:, 2:4]


def _ssd_mixer(proj, dt_raw, p, l, dims, fin_prev):
    n_cb, n_cs, n_lb, n_ls = dims
    t_ctx = n_cb * n_cs
    t_all = proj.shape[0]
    pad_heads = jnp.zeros((SSD_NGROUPS, LANES - 2 * SSD_HPG), F32)

    def per_group(a):
        return jnp.concatenate([a[0].reshape(SSD_NGROUPS, SSD_HPG), a[1].reshape(SSD_NGROUPS, SSD_HPG),
                                pad_heads], axis=1)[:, None, :]

    dtb_rows = per_group(p['ssd_dt_bias'][l])
    a_rows = per_group(-jnp.exp(p['ssd_a_log'][l]))
    d_rows = jnp.repeat(p['ssd_d'][l], SSD_HEADDIM).reshape(SSD_NGROUPS, 1, SSD_HPG * SSD_HEADDIM)
    dtr = dt_raw.reshape(t_all, 2, SSD_NGROUPS, SSD_HPG)
    dtg = jnp.concatenate([jnp.transpose(dtr, (2, 0, 1, 3)).reshape(SSD_NGROUPS, t_all, 2 * SSD_HPG),
                           jnp.zeros((SSD_NGROUPS, t_all, LANES - 2 * SSD_HPG), F32)], axis=-1)
    cw, cb = p['ssd_conv_w'][l], p['ssd_conv_b'][l][None]
    y, fin = _ssd_scan(proj, dtg, cw, cb, dtb_rows, a_rows, d_rows, None, None, fin_prev,
                       layer=l, row0=0, n_b=n_cb, seq=n_cs, seg=n_cs, want_fin=True)
    y, _ = _ssd_scan(proj, dtg, cw, cb, dtb_rows, a_rows, d_rows, p['state_ssd'][:, l], y, None,
                     layer=l, row0=t_ctx, n_b=n_lb, seq=n_ls, seg=GRID_W, want_fin=False)
    return _ssd_norm(y, proj, p['ssd_norm'][l][None]), fin


def _hyena_mixer(proj, p, l, dims, dfts):
    n_cb, n_cs, n_lb, n_ls = dims
    hcw, hcb = p['hy_conv_w'][l], p['hy_conv_b'][l][None]
    out = None
    cfgs = ((0, n_cb, n_cs, n_cs, 512), (n_cb * n_cs, n_lb, n_ls, GRID_W, 256))
    for (row0, nb, seq, seg, tc), dft in zip(cfgs, dfts):
        hsum, hdiff = _hyena_filters(seq, p['hy_w1'][l], p['hy_b1'][l], p['hy_w2'][l], p['hy_b2'][l], p['hy_w3'][l],
                                     p['hy_freq'][l])
        out = _hyena(proj, hcw, hcb, _hyena_spectra(hsum, hdiff, dft, seq), p['hy_bias'][l], dft, out,
                     n_rows=proj.shape[0], row0=row0, n_b=nb, seq=seq, seg=seg, tc=tc)
    return out


def _in_proj_body(a_ref, wt_ref, o_ref, wbf_ref):
    @pl.when(pl.program_id(1) == 0)
    def _():
        wbf_ref[...] = wt_ref[...].astype(BF16)

    o_ref[...] = lax.dot_general(a_ref[...], wbf_ref[...], (((1,), (1,)), ((), ())), preferred_element_type=F32)


def _in_proj(h, w_in, l, tm=1024):
    m, d = h.shape
    assert N_IN % 8 == 0 and OFF_HY % 8 == 0
    wt = jnp.swapaxes(w_in, 1, 2).reshape(w_in.shape[0] * N_IN, d)
    n_main, n_hy = OFF_DT // PROJ_TN, (N_IN - OFF_HY) // PROJ_TN
    assert OFF_DT % PROJ_TN == 0 and (N_IN - OFF_HY) % PROJ_TN == 0 and OFF_DT + PROJ_TN <= N_IN
    assert (n_main + n_hy + 1) * PROJ_TN == N_PROJ and m % tm == 0

    def w_row(j):
        return jnp.where(j < n_main, j * PROJ_TN, jnp.where(j < n_main + n_hy, OFF_HY + (j - n_main) * PROJ_TN, OFF_DT))

    return pl.pallas_call(
        _in_proj_body,
        grid=(N_PROJ // PROJ_TN, m // tm),
        in_specs=[pl.BlockSpec((tm, d), lambda j, i: (i, 0)),
                  pl.BlockSpec((pl.Element(PROJ_TN), pl.Element(d)),
                               lambda j, i: (pl.multiple_of(l * N_IN + w_row(j), 8), 0))],
        out_specs=pl.BlockSpec((tm, PROJ_TN), lambda j, i: (i, j)),
        out_shape=jax.ShapeDtypeStruct((m, N_PROJ), F32),
        scratch_shapes=[pltpu.VMEM((PROJ_TN, d), BF16)],
        compiler_params=_cparams(("arbitrary", "arbitrary")),
        name="w_in",
    )(h, wt)


def _moe(h_packed, route, wg, wu, wd, lead):
    t_all = h_packed.shape[0]
    token_of_row, row_of_pair, item_e, item_b, item_rows = _moe_dispatch(route, t_all)
    xs = jnp.take(h_packed, token_of_row, axis=0, mode="clip")
    ys = _ffn(xs, wg, wu, wd, lead, item_e, item_b, item_rows, packed=True, row_variants=FFN_MOE_ROWS)
    return (jnp.take(ys, row_of_pair[:, 0], axis=0, mode="clip"),
            jnp.take(ys, row_of_pair[:, 1], axis=0, mode="clip"))


def kernel(x_prompt, x_sample, state_s5_re, state_s5_im, state_ssd, c, c_ctx, ada_w, ada_b, norm1, norm2, final_norm, w_in, w_out, s5_a_re, s5_a_im, s5_log_dt, s5_b_re, s5_b_im, s5_c_re, s5_c_im, s5_d, s5_w_glu, ssd_conv_w, ssd_conv_b, ssd_dt_bias, ssd_a_log, ssd_d, ssd_norm, hy_conv_w, hy_conv_b, hy_w1, hy_b1, hy_w2, hy_b2, hy_w3, hy_freq, hy_bias, ffn_wg, ffn_wu, ffn_wd, moe_router, moe_wg, moe_wu, moe_wd):
    n_cb, n_cs, d = x_prompt.shape
    n_lb, n_ls, _ = x_sample.shape
    assert d == D_MODEL and n_ls == LAT_ROWS and n_cs % SSD_CHUNK == 0 and n_ls % SSD_CHUNK == 0
    t_ctx, t_lat = n_cb * n_cs, n_lb * n_ls
    t_all = t_ctx + t_lat
    assert t_ctx % LAT_ROWS == 0 and t_all % FFN_ROWS == 0 and 1 + n_lb <= 3
    x = jnp.concatenate([x_prompt.reshape(t_ctx, d), x_sample.reshape(t_lat, d)], axis=0)

    cond = jnp.zeros((8, d), F32).at[0].set(c_ctx).at[1:1 + n_lb].set(c)
    mod = jnp.concatenate(
        [_matmul(cond, ada_w, l, 6 * d, tm=8, tn=512, out_dtype=F32, a_silu=True,
                 bias=ada_b.reshape(DEPTH, 1, 6 * d), name="ada_mod")[None] for l in range(DEPTH)], axis=0)
    mod3 = mod[:, :3].reshape(DEPTH, 3, 1, 6 * d)

    p = dict(state_s5_re=state_s5_re, state_s5_im=state_s5_im, state_ssd=state_ssd,
             s5_a_re=s5_a_re, s5_a_im=s5_a_im, s5_log_dt=s5_log_dt, s5_b_re=s5_b_re, s5_b_im=s5_b_im,
             s5_c_re=s5_c_re, s5_c_im=s5_c_im, s5_d=s5_d, s5_w_glu=s5_w_glu,
             ssd_conv_w=ssd_conv_w, ssd_conv_b=ssd_conv_b, ssd_dt_bias=ssd_dt_bias, ssd_a_log=ssd_a_log,
             ssd_d=ssd_d, ssd_norm=ssd_norm, hy_conv_w=hy_conv_w, hy_conv_b=hy_conv_b, hy_w1=hy_w1, hy_b1=hy_b1,
             hy_w2=hy_w2, hy_b2=hy_b2, hy_w3=hy_w3, hy_freq=hy_freq, hy_bias=hy_bias)
    dims = (n_cb, n_cs, n_lb, n_ls)
    dfts = (_dft_ops(n_cs), _dft_ops(n_ls))
    s5_ops = jax.vmap(_s5_prep)(s5_a_re, s5_a_im, s5_log_dt, s5_b_re, s5_b_im, s5_c_re, s5_c_im, s5_d)
    fin_re, fin_im, fin_ssd = [], [], None
    for l in range(DEPTH):
        m3 = mod3[l]
        h = _norm_mod(x, norm1[l][None], m3, 1, 0, t_ctx)
        proj = _in_proj(h, w_in, l)
        dt_raw = proj[:, PROJ_DT:PROJ_DT + 2 * SSD_HEADS]
        y_s5, f_re, f_im = _s5_mixer(proj, p, tuple(o[l] for o in s5_ops), l, dims)
        y_ssd, fin_ssd = _ssd_mixer(proj, dt_raw, p, l, dims, fin_ssd)
        y_hy = _hyena_mixer(proj, p, l, dims, dfts)
        fin_re.append(f_re)
        fin_im.append(f_im)
        x = _matmul([y_s5, y_ssd, y_hy], w_out, l, d, tm=1024, tn=512, out_dtype=F32, resid=x,
                    gate=m3[:, :, 2 * d:3 * d], n_ctx_rows=t_ctx, name="w_out")
        final_w = final_norm[None] if l == DEPTH - 1 else None
        if l % 2 == 0:
            h = _norm_mod(x, norm2[l][None], m3, 4, 3, t_ctx)
            n_items = t_all // FFN_ROWS
            f = _ffn(h, ffn_wg, ffn_wu, ffn_wd, l // 2, jnp.zeros((n_items,), jnp.int32),
                     jnp.arange(n_items, dtype=jnp.int32), jnp.full((n_items,), FFN_ROWS, jnp.int32),
                     packed=False, row_variants=FFN_DENSE_ROWS)
            x = _resid(x, [f], m3, 5, t_ctx, final_w=final_w)
        else:
            router = jnp.concatenate([moe_router[l // 2], jnp.zeros((d, LANES - N_EXPERTS), F32)], axis=1)
            h_packed, route = _norm_mod(x, norm2[l][None], m3, 4, 3, t_ctx, router=router)
            x = _resid(x, _moe(h_packed, route, moe_wg, moe_wu, moe_wd, l // 2), m3, 5, t_ctx, route=route,
                       final_w=final_w)

    y_prompt = x[:t_ctx].reshape(n_cb, n_cs, d)
    y_sample = x[t_ctx:].reshape(n_lb, n_ls, d)
    return (y_prompt, y_sample, jnp.stack(fin_re, axis=1), jnp.stack(fin_im, axis=1), fin_ssd)
```

```python
import functools
import math

import jax
import jax.numpy as jnp
from jax import lax
from jax.experimental import pallas as pl
from jax.experimental.pallas import tpu as pltpu

F32 = jnp.float32
BF16 = jnp.bfloat16
HIGHEST = lax.Precision.HIGHEST

D_MODEL = 4096
DEPTH = 2
GRID_W = 64
EPS = 1e-6
D_S5 = D_MODEL // 4
D_SSD = D_MODEL // 2
D_HY = D_MODEL - D_S5 - D_SSD
S5_H = 16
S5_GROUPS = D_S5 // S5_H
S5_P = 64
S5_Q = 16
SSD_HEADDIM = 64
SSD_HEADS = D_SSD // SSD_HEADDIM
SSD_NGROUPS = 4
SSD_HPG = SSD_HEADS // SSD_NGROUPS
SSD_DSTATE = 128
SSD_CHUNK = 128
SSD_CONV_DIM = D_SSD + 2 * SSD_NGROUPS * SSD_DSTATE
HY_ORDER = 2
HY_BANDS = 16
HY_MIN_DECAY = math.log(1e-2) / 1.5
HY_MAX_DECAY = math.log(1e-2) / 0.3
D_FF = 14336
N_EXPERTS = 8
OFF_Z = D_S5
OFF_XBC = OFF_Z + D_SSD
OFF_DT = OFF_XBC + SSD_CONV_DIM
OFF_HY = OFF_DT + 2 * SSD_HEADS
N_IN = OFF_HY + (HY_ORDER + 1) * D_HY

LANES = 128
VMEM_LIMIT = 56 * 1024 * 1024
LAT_ROWS = 1024
PROJ_TN = 512
N_PROJ = -(-N_IN // PROJ_TN) * PROJ_TN
PROJ_HY = OFF_DT
PROJ_DT = PROJ_HY + (HY_ORDER + 1) * D_HY


def _cparams(sem, vmem=VMEM_LIMIT):
    return pltpu.CompilerParams(dimension_semantics=sem, vmem_limit_bytes=vmem)


def _mod_row(i, tm, n_ctx_rows):
    return jnp.maximum((i * tm - n_ctx_rows) // LAT_ROWS + 1, 0)


def _mm_body(*refs, n_a, a_silu, has_bias, has_resid):
    a_refs, w_ref = refs[:n_a], refs[n_a]
    k = n_a + 1
    bias_ref = resid_ref = gate_ref = None
    if has_bias:
        bias_ref = refs[k]
        k += 1
    if has_resid:
        resid_ref, gate_ref = refs[k], refs[k + 1]
        k += 2
    o_ref, wbf_ref = refs[k], refs[k + 1]

    @pl.when(pl.program_id(1) == 0)
    def _():
        wbf_ref[...] = w_ref[...].astype(BF16)

    acc = None
    k0 = 0
    for a_ref in a_refs:
        a = a_ref[...]
        if a_silu:
            a = jax.nn.silu(a).astype(BF16)
        part = jnp.dot(a, wbf_ref[k0:k0 + a.shape[1], :], preferred_element_type=F32)
        acc = part if acc is None else acc + part
        k0 += a.shape[1]
    if has_bias:
        acc = acc + bias_ref[...]
    if has_resid:
        acc = resid_ref[...] + gate_ref[...] * acc
    o_ref[...] = acc.astype(o_ref.dtype)


def _matmul(a, w, w_lead, n_cols, *, tm, tn, out_dtype, a_silu=False, bias=None, resid=None, gate=None,
            n_ctx_rows=0, name="matmul"):
    a_list = list(a) if isinstance(a, (list, tuple)) else [a]
    m = a_list[0].shape[0]
    kdim = sum(x.shape[1] for x in a_list)
    assert m % tm == 0 and n_cols % tn == 0 and kdim == w.shape[-2]
    grid = (n_cols // tn, m // tm)
    w_spec = pl.BlockSpec((None, kdim, tn), lambda j, i: (w_lead, 0, j))
    in_specs = [pl.BlockSpec((tm, x.shape[1]), lambda j, i: (i, 0)) for x in a_list] + [w_spec]
    args = a_list + [w]
    if bias is not None:
        in_specs.append(pl.BlockSpec((None, 1, tn), lambda j, i: (w_lead, 0, j)))
        args.append(bias)
    if resid is not None:
        in_specs.append(pl.BlockSpec((tm, tn), lambda j, i: (i, j)))
        in_specs.append(pl.BlockSpec((None, 1, tn), lambda j, i: (_mod_row(i, tm, n_ctx_rows), 0, j)))
        args += [resid, gate]
    return pl.pallas_call(
        functools.partial(_mm_body, n_a=len(a_list), a_silu=a_silu, has_bias=bias is not None,
                          has_resid=resid is not None),
        grid=grid,
        in_specs=in_specs,
        out_specs=pl.BlockSpec((tm, tn), lambda j, i: (i, j)),
        out_shape=jax.ShapeDtypeStruct((m, n_cols), out_dtype),
        scratch_shapes=[pltpu.VMEM((kdim, tn), BF16)],
        compiler_params=_cparams(("arbitrary", "arbitrary")),
        name=name,
    )(*args)


def _norm_mod_body(x_ref, w_ref, sc_ref, sh_ref, *rest, with_router):
    x = x_ref[...]
    ms = jnp.mean(x * x, axis=-1, keepdims=True)
    y = x * lax.rsqrt(ms + EPS) * w_ref[...]
    h = y * (1.0 + sc_ref[...]) + sh_ref[...]
    if not with_router:
        (o_ref,) = rest
        o_ref[...] = h.astype(o_ref.dtype)
        return
    r_ref, o_ref, route_ref = rest
    half = h.shape[1] // 2
    hb = h.astype(BF16)
    lo = lax.bitcast_convert_type(hb[:, :half].astype(F32), jnp.uint32)
    hi = lax.bitcast_convert_type(hb[:, half:].astype(F32), jnp.uint32)
    o_ref[...] = lax.bitcast_convert_type((lo >> 16) | hi, F32)
    logits = jnp.dot(h, r_ref[...], preferred_element_type=F32, precision=HIGHEST)
    lane = lax.broadcasted_iota(jnp.int32, logits.shape, 1).astype(F32)
    neg = jnp.float32(-jnp.inf)
    lg = jnp.where(lane < N_EXPERTS, logits, neg)
    m1 = jnp.max(lg, axis=-1, keepdims=True)
    i1 = jnp.min(jnp.where(lg == m1, lane, float(LANES)), axis=-1, keepdims=True)
    lg2 = jnp.where(lane == i1, neg, lg)
    m2 = jnp.max(lg2, axis=-1, keepdims=True)
    i2 = jnp.min(jnp.where(lg2 == m2, lane, float(LANES)), axis=-1, keepdims=True)
    e = jnp.exp(m2 - m1)
    g1 = 1.0 / (1.0 + e)
    g2 = e / (1.0 + e)
    route = jnp.where(lane == 0, i1, jnp.where(lane == 1, i2, jnp.where(lane == 2, g1, jnp.where(lane == 3, g2, 0.0))))
    route_ref[...] = route


def _norm_mod(x, w_row, mod3, k_scale, k_shift, n_ctx_rows, router=None, tm=256):
    m, d = x.shape
    in_specs = [
        pl.BlockSpec((tm, d), lambda i: (i, 0)),
        pl.BlockSpec((1, d), lambda i: (0, 0)),
        pl.BlockSpec((None, 1, d), lambda i: (_mod_row(i, tm, n_ctx_rows), 0, k_scale)),
        pl.BlockSpec((None, 1, d), lambda i: (_mod_row(i, tm, n_ctx_rows), 0, k_shift)),
    ]
    args = [x, w_row, mod3, mod3]
    out_specs = pl.BlockSpec((tm, d), lambda i: (i, 0))
    out_shape = jax.ShapeDtypeStruct((m, d), BF16)
    if router is not None:
        in_specs.append(pl.BlockSpec((d, LANES), lambda i: (0, 0)))
        args.append(router)
        out_specs = [pl.BlockSpec((tm, d // 2), lambda i: (i, 0)), pl.BlockSpec((tm, LANES), lambda i: (i, 0))]
        out_shape = [jax.ShapeDtypeStruct((m, d // 2), F32), jax.ShapeDtypeStruct((m, LANES), F32)]
    return pl.pallas_call(
        functools.partial(_norm_mod_body, with_router=router is not None),
        grid=(m // tm,),
        in_specs=in_specs,
        out_specs=out_specs,
        out_shape=out_shape,
        compiler_params=_cparams(("arbitrary",)),
        name="norm_mod",
    )(*args)


def _resid_body(*refs, n_f, final):
    x_ref, f_refs, g_ref = refs[0], refs[1:1 + n_f], refs[1 + n_f]
    rest = refs[2 + n_f:]
    if n_f == 1:
        f = f_refs[0][...]
    else:
        route_ref, rest = rest[0], rest[1:]
        f = route_ref[:, 2:3] * f_refs[0][...] + route_ref[:, 3:4] * f_refs[1][...]
    y = x_ref[...] + g_ref[...] * f
    if final:
        w_ref, o_ref = rest
        ms = jnp.mean(y * y, axis=-1, keepdims=True)
        y = y * lax.rsqrt(ms + EPS) * w_ref[...]
    else:
        (o_ref,) = rest
    o_ref[...] = y


def _resid(x, fs, mod3, k_gate, n_ctx_rows, route=None, final_w=None, tm=256):
    m, d = x.shape
    tok = pl.BlockSpec((tm, d), lambda i: (i, 0))
    in_specs = [tok] + [tok] * len(fs) + [
        pl.BlockSpec((None, 1, d), lambda i: (_mod_row(i, tm, n_ctx_rows), 0, k_gate))]
    args = [x, *fs, mod3]
    if route is not None:
        in_specs.append(pl.BlockSpec((tm, LANES), lambda i: (i, 0)))
        args.append(route)
    if final_w is not None:
        in_specs.append(pl.BlockSpec((1, d), lambda i: (0, 0)))
        args.append(final_w)
    return pl.pallas_call(
        functools.partial(_resid_body, n_f=len(fs), final=final_w is not None),
        grid=(m // tm,),
        in_specs=in_specs,
        out_specs=pl.BlockSpec((tm, d), lambda i: (i, 0)),
        out_shape=jax.ShapeDtypeStruct((m, d), F32),
        compiler_params=_cparams(("arbitrary",)),
        name="resid",
    )(*args)


FFN_ROWS = 1024
FFN_TF = 256
FFN_KC = 1024
FFN_NC = 512
FFN_VMEM_LIMIT = 62 * 1024 * 1024
FFN_DENSE_ROWS = (FFN_ROWS,)
FFN_MOE_ROWS = (256, 512, 768, FFN_ROWS)


def _ffn_body(ie_ref, ib_ref, nv_ref, x_ref, wg_ref, wu_ref, wd_ref, o_ref, *, packed, row_variants):
    it = pl.program_id(0)
    f = pl.program_id(1)
    nv = nv_ref[it]
    d = wg_ref.shape[0]
    gran = row_variants[0]

    @pl.when(f == 0)
    def _():
        o_ref[...] = jnp.zeros_like(o_ref)

    def x_chunk(m, kc):
        if not packed:
            return x_ref[0:m, kc * FFN_KC:(kc + 1) * FFN_KC]
        per_half = (d // 2) // FFN_KC
        c0 = (kc % per_half) * FFN_KC
        w = lax.bitcast_convert_type(x_ref[0:m, c0:c0 + FFN_KC], jnp.uint32)
        bits = (w << 16) if kc < per_half else (w & jnp.uint32(0xFFFF0000))
        return lax.bitcast_convert_type(bits, F32).astype(BF16)

    def compute(m):
        g = u = None
        for kc in range(d // FFN_KC):
            ks = slice(kc * FFN_KC, (kc + 1) * FFN_KC)
            xk = x_chunk(m, kc)
            pg = jnp.dot(xk, wg_ref[ks, :].astype(BF16), preferred_element_type=F32)
            pu = jnp.dot(xk, wu_ref[ks, :].astype(BF16), preferred_element_type=F32)
            g = pg if g is None else g + pg
            u = pu if u is None else u + pu
        a = (jax.nn.silu(g) * u).astype(BF16)
        for nc in range(d // FFN_NC):
            ns = slice(nc * FFN_NC, (nc + 1) * FFN_NC)
            o_ref[0:m, ns] += jnp.dot(a, wd_ref[:, ns].astype(BF16), preferred_element_type=F32)

    n_gran = (nv + gran - 1) // gran
    for m in row_variants:
        @pl.when(n_gran == m // gran)
        def _(m=m):
            compute(m)


def _ffn(xs, wg, wu, wd, lead, item_expert, item_block, item_rows, *, packed, row_variants):
    r = xs.shape[0]
    d = wg.shape[-2]
    n_items = item_expert.shape[0]
    dff = wg.shape[-1]
    n_f = dff // FFN_TF
    assert all(m % row_variants[0] == 0 for m in row_variants) and row_variants[-1] == FFN_ROWS

    def f_idx(it, f, nv):
        return jnp.where(nv[it] > 0, f, n_f - 1)

    if wg.ndim == 4:
        wgu_spec = pl.BlockSpec((None, None, d, FFN_TF), lambda it, f, ie, ib, nv: (lead, ie[it], 0, f_idx(it, f, nv)))
        wd_spec = pl.BlockSpec((None, None, FFN_TF, d), lambda it, f, ie, ib, nv: (lead, ie[it], f_idx(it, f, nv), 0))
    else:
        wgu_spec = pl.BlockSpec((None, d, FFN_TF), lambda it, f, ie, ib, nv: (lead, 0, f_idx(it, f, nv)))
        wd_spec = pl.BlockSpec((None, FFN_TF, d), lambda it, f, ie, ib, nv: (lead, f_idx(it, f, nv), 0))
    grid_spec = pltpu.PrefetchScalarGridSpec(
        num_scalar_prefetch=3,
        grid=(n_items, n_f),
        in_specs=[
            pl.BlockSpec((FFN_ROWS, xs.shape[1]), lambda it, f, ie, ib, nv: (ib[it], 0),
                         pipeline_mode=pl.Buffered(1)),
            wgu_spec, wgu_spec, wd_spec,
        ],
        out_specs=pl.BlockSpec((FFN_ROWS, d), lambda it, f, ie, ib, nv: (ib[it], 0), pipeline_mode=pl.Buffered(1)),
    )
    return pl.pallas_call(
        functools.partial(_ffn_body, packed=packed, row_variants=row_variants),
        grid_spec=grid_spec,
        out_shape=jax.ShapeDtypeStruct((r, d), F32),
        compiler_params=_cparams(("arbitrary", "arbitrary"), vmem=FFN_VMEM_LIMIT),
        name="ffn",
    )(item_expert, item_block, item_rows, xs, wg, wu, wd)


def _moe_dispatch(route, n_tokens):
    e_idx = route[:, :2].astype(jnp.int32)
    flat_e = e_idx.reshape(-1)
    n_pairs = flat_e.shape[0]
    onehot = (flat_e[:, None] == jnp.arange(N_EXPERTS)[None, :]).astype(jnp.int32)
    seen = jnp.cumsum(onehot, axis=0)
    counts = seen[-1]
    rank = jnp.sum(onehot * seen, axis=1) - 1
    blocks_per = (counts + FFN_ROWS - 1) // FFN_ROWS
    blk_end = jnp.cumsum(blocks_per)
    blk_start = blk_end - blocks_per
    dest = jnp.sum(onehot * blk_start[None, :], axis=1) * FFN_ROWS + rank
    n_items = (n_pairs + N_EXPERTS * (FFN_ROWS - 1)) // FFN_ROWS
    n_rows = n_items * FFN_ROWS
    token_of_row = jnp.zeros((n_rows,), jnp.int32).at[dest].set(jnp.arange(n_pairs, dtype=jnp.int32) // 2)
    row_of_pair = dest.reshape(n_tokens, 2)
    blk = jnp.arange(n_items, dtype=jnp.int32)
    n_real = blk_end[-1]
    blk_c = jnp.minimum(blk, n_real - 1)
    item_e = jnp.minimum(jnp.sum(blk_c[:, None] >= blk_end[None, :], axis=1), N_EXPERTS - 1).astype(jnp.int32)
    rows_left = counts[item_e] - (blk_c - blk_start[item_e]) * FFN_ROWS
    item_rows = jnp.where(blk < n_real, jnp.clip(rows_left, 0, FFN_ROWS), 0).astype(jnp.int32)
    return token_of_row, row_of_pair, item_e, blk, item_rows


def _s5_prep(a_re, a_im, log_dt, b_re, b_im, c_re, c_im, d):
    q, g, p, h = S5_Q, S5_GROUPS, S5_P, S5_H
    dt = jnp.exp(log_dt)[..., None]
    k = jnp.arange(q + 1, dtype=F32)
    mag = jnp.exp((a_re * dt)[..., None] * k)
    ang = (a_im * dt)[..., None] * k
    pw_re, pw_im = mag * jnp.cos(ang), mag * jnp.sin(ang)
    ab_re, ab_im = pw_re[..., 1], pw_im[..., 1]
    den = a_re * a_re + a_im * a_im
    q_re = ((ab_re - 1.0) * a_re + ab_im * a_im) / den
    q_im = (ab_im * a_re - (ab_re - 1.0) * a_im) / den
    bb_re = q_re[..., None] * b_re[None] - q_im[..., None] * b_im[None]
    bb_im = q_re[..., None] * b_im[None] + q_im[..., None] * b_re[None]
    c_re_t = jnp.swapaxes(c_re, 1, 2)[None, ..., None]
    c_im_t = jnp.swapaxes(c_im, 1, 2)[None, ..., None]
    cp_re = c_re_t * pw_re[:, :, :, None, :] - c_im_t * pw_im[:, :, :, None, :]
    cp_im = c_re_t * pw_im[:, :, :, None, :] + c_im_t * pw_re[:, :, :, None, :]
    cpx_re = jnp.swapaxes(cp_re, 3, 4)[..., None]
    cpx_im = jnp.swapaxes(cp_im, 3, 4)[..., None]
    bbx_re = bb_re[:, :, :, None, None, :]
    bbx_im = bb_im[:, :, :, None, None, :]
    kern = jnp.transpose(jnp.sum(cpx_re * bbx_re - cpx_im * bbx_im, axis=2), (0, 2, 1, 3, 4))
    ii = jnp.arange(q)[:, None]
    jj = jnp.arange(q)[None, :]
    lag = ii - jj
    kf = jnp.where((lag >= 0)[..., None, None, None], kern[0][jnp.clip(lag, 0, q)], 0.0)
    kb = jnp.where((lag <= 0)[..., None, None, None], kern[1][jnp.clip(-lag, 0, q)], 0.0)
    m_intra = jnp.transpose(kf + kb, (2, 1, 4, 0, 3)).reshape(g, q * h, q * h)
    pf_re, pf_im = pw_re[0][..., q - 1 - jnp.arange(q)], pw_im[0][..., q - 1 - jnp.arange(q)]
    pb_re, pb_im = pw_re[1][..., :q], pw_im[1][..., :q]

    def contrib(p_re, p_im, b_r, b_i):
        w_re = p_re[:, :, :, None] * b_r[:, :, None, :] - p_im[:, :, :, None] * b_i[:, :, None, :]
        w_im = p_re[:, :, :, None] * b_i[:, :, None, :] + p_im[:, :, :, None] * b_r[:, :, None, :]
        to_rows = lambda w: jnp.transpose(w, (0, 2, 3, 1)).reshape(g, q * h, p)
        return to_rows(w_re), to_rows(w_im)

    wsf_re, wsf_im = contrib(pf_re, pf_im, bb_re[0], bb_im[0])
    wsb_re, wsb_im = contrib(pb_re, pb_im, bb_re[1], bb_im[1])
    w_state = jnp.concatenate([wsf_re, wsb_re, wsf_im, wsb_im], axis=-1)
    to_cols = lambda w: jnp.transpose(w, (0, 1, 3, 2)).reshape(g, p, q * h)
    fi = jnp.arange(q) + 1
    bi = q - jnp.arange(q)
    w_off = jnp.concatenate([to_cols(cp_re[0][..., fi]), to_cols(cp_re[1][..., bi]),
                             to_cols(-cp_im[0][..., fi]), to_cols(-cp_im[1][..., bi])], axis=1)
    aq_re, aq_im = pw_re[..., q], pw_im[..., q]
    a1 = jnp.concatenate([aq_re[0], aq_re[1]], axis=-1)[:, None, :]
    a2 = jnp.concatenate([aq_im[0], aq_im[1]], axis=-1)[:, None, :]
    d_row = jnp.tile(d.reshape(g, 1, h), (1, 1, q))
    return m_intra.astype(BF16), w_state.astype(BF16), w_off.astype(BF16), a1, a2, d_row


def _s5_body(u_ref, mi_ref, ws_ref, wo_ref, a1_ref, a2_ref, d_ref, x0_ref, y_ref, fin_ref, con_ref, sp_ref,
             *, n_ctx_b, n_ctx_c, n_lat_b, n_lat_c):
    half = 2 * S5_P
    ub = u_ref[...]
    u = ub.astype(F32)
    con_ref[...] = jnp.dot(ub, ws_ref[...], preferred_element_type=F32)

    def scan(row0, nb, nc, s_re, s_im):
        ar = jnp.broadcast_to(a1_ref[...], (nb, half))
        ai = jnp.broadcast_to(a2_ref[...], (nb, half))
        is_fwd = lax.broadcasted_iota(jnp.int32, (nb, half), 1) < S5_P
        for k in range(nc):
            rf = slice(row0 + k * nb, row0 + (k + 1) * nb)
            rb = slice(row0 + (nc - 1 - k) * nb, row0 + (nc - k) * nb)
            sp_ref[rf, 0:S5_P] = s_re[:, :S5_P]
            sp_ref[rb, S5_P:half] = s_re[:, S5_P:]
            sp_ref[rf, half:half + S5_P] = s_im[:, :S5_P]
            sp_ref[rb, half + S5_P:2 * half] = s_im[:, S5_P:]
            c_re = jnp.where(is_fwd, con_ref[rf, 0:half], con_ref[rb, 0:half])
            c_im = jnp.where(is_fwd, con_ref[rf, half:2 * half], con_ref[rb, half:2 * half])
            s_re, s_im = ar * s_re - ai * s_im + c_re, ar * s_im + ai * s_re + c_im
        return s_re, s_im

    zeros = jnp.zeros((n_ctx_b, half), F32)
    fin_ref[:, 0:half], fin_ref[:, half:2 * half] = scan(0, n_ctx_b, n_ctx_c, zeros, zeros)
    scan(n_ctx_b * n_ctx_c, n_lat_b, n_lat_c, x0_ref[:, 0:half], x0_ref[:, half:2 * half])
    y = jnp.dot(ub, mi_ref[...], preferred_element_type=F32)
    y = y + jnp.dot(sp_ref[...].astype(BF16), wo_ref[...], preferred_element_type=F32)
    y_ref[...] = (y + u * d_ref[...]).astype(y_ref.dtype)


def _s5_scan(u_rows, ops, x0, n_ctx_b, n_ctx_c, n_lat_b, n_lat_c):
    g, r, w = u_rows.shape
    m_intra, w_state, w_off, a1, a2, d_row = ops
    mat = pl.BlockSpec((None, w, w), lambda i: (i, 0, 0))
    row = pl.BlockSpec((None, 1, w), lambda i: (i, 0, 0))
    return pl.pallas_call(
        functools.partial(_s5_body, n_ctx_b=n_ctx_b, n_ctx_c=n_ctx_c, n_lat_b=n_lat_b, n_lat_c=n_lat_c),
        grid=(g,),
        in_specs=[pl.BlockSpec((None, r, w), lambda i: (i, 0, 0)), mat, mat, mat,
                  pl.BlockSpec((None, 1, w // 2), lambda i: (i, 0, 0)),
                  pl.BlockSpec((None, 1, w // 2), lambda i: (i, 0, 0)), row,
                  pl.BlockSpec((None, n_lat_b, w), lambda i: (i, 0, 0))],
        out_specs=[pl.BlockSpec((None, r, w), lambda i: (i, 0, 0)),
                   pl.BlockSpec((None, n_ctx_b, w), lambda i: (i, 0, 0))],
        out_shape=[jax.ShapeDtypeStruct((g, r, w), BF16), jax.ShapeDtypeStruct((g, n_ctx_b, w), F32)],
        scratch_shapes=[pltpu.VMEM((r, w), F32), pltpu.VMEM((r, w), F32)],
        compiler_params=_cparams(("arbitrary",)),
        name="s5_scan",
    )(u_rows, m_intra, w_state, w_off, a1, a2, d_row, x0)


def _s5_glu_body(y_ref, w_ref, o_ref, wbf_ref):
    @pl.when(pl.program_id(0) == 0)
    def _():
        wbf_ref[...] = w_ref[...].astype(BF16)

    y = jax.nn.gelu(y_ref[...].astype(F32))
    z = jnp.dot(y.astype(BF16), wbf_ref[...], preferred_element_type=F32)
    o_ref[...] = (y * jax.nn.sigmoid(z)).astype(o_ref.dtype)


def _s5_glu(y, w_glu, lead, tm=512):
    m, d = y.shape
    return pl.pallas_call(
        _s5_glu_body,
        grid=(m // tm,),
        in_specs=[pl.BlockSpec((tm, d), lambda i: (i, 0)), pl.BlockSpec((None, d, d), lambda i: (lead, 0, 0))],
        out_specs=pl.BlockSpec((tm, d), lambda i: (i, 0)),
        out_shape=jax.ShapeDtypeStruct((m, d), BF16),
        scratch_shapes=[pltpu.VMEM((d, d), BF16)],
        compiler_params=_cparams(("arbitrary",)),
        name="s5_glu",
    )(y, w_glu)


def _conv3(u, w_ref, b_ref, seg):
    n = u.shape[0]
    assert seg & (seg - 1) == 0
    pos = lax.broadcasted_iota(jnp.int32, u.shape, 0) & (seg - 1)
    prev = jnp.where(pos == 0, 0.0, pltpu.roll(u, 1, 0))
    nxt = jnp.where(pos == seg - 1, 0.0, pltpu.roll(u, n - 1, 0))
    return b_ref[...] + w_ref[0:1, :] * prev + w_ref[1:2, :] * u + w_ref[2:3, :] * nxt


def _softplus(x):
    return jnp.maximum(x, 0.0) + jnp.log(1.0 + jnp.exp(-jnp.abs(x)))


def _ssd_body(*refs, seq, seg, has_init, want_fin, n_aliased):
    (x_ref, b_ref, c_ref, dt_ref, cwx_ref, cwb_ref, cwc_ref, cbx_ref, cbb_ref, cbc_ref,
     dtb_ref, arow_ref, drow_ref) = refs[:13]
    k = 13
    init_ref = None
    if has_init:
        init_ref = refs[k]
        k += 1
    k += n_aliased
    y_ref = refs[k]
    k += 1
    fin_ref = None
    if want_fin:
        fin_ref = refs[k]
        k += 1
    xc_ref, bc_ref, cc_ref, stf_ref, stb_ref, xsb_ref, rcsx_ref = refs[k:k + 7]
    q, p, hpg = SSD_CHUNK, SSD_HEADDIM, SSD_HPG
    wid = hpg * p
    nc = seq // q

    xc_ref[...] = jax.nn.silu(_conv3(x_ref[...], cwx_ref, cbx_ref, seg))
    bc_ref[...] = jax.nn.silu(_conv3(b_ref[...], cwb_ref, cbb_ref, seg))
    cc_ref[...] = jax.nn.silu(_conv3(c_ref[...], cwc_ref, cbc_ref, seg))
    if has_init:
        stf_ref[...] = jnp.concatenate([init_ref[0, r] for r in range(hpg)], axis=0).T
        stb_ref[...] = jnp.concatenate([init_ref[1, r] for r in range(hpg)], axis=0).T
    else:
        stf_ref[...] = jnp.zeros_like(stf_ref)
        stb_ref[...] = jnp.zeros_like(stb_ref)

    ii = lax.broadcasted_iota(jnp.int32, (q, q), 0)
    jj = lax.broadcasted_iota(jnp.int32, (q, q), 1)
    tri_lo = (ii >= jj).astype(BF16)
    er = lax.broadcasted_iota(jnp.int32, (LANES, 2 * wid), 0)
    ec = lax.broadcasted_iota(jnp.int32, (LANES, 2 * wid), 1) // p
    expand = (er == ec).astype(BF16)
    neg = jnp.float32(-jnp.inf)
    bdot = functools.partial(jnp.dot, preferred_element_type=F32)

    def split3(v):
        v1 = v.astype(BF16)
        r1 = v - v1.astype(F32)
        v2 = r1.astype(BF16)
        return v1, v2, (r1 - v2.astype(F32)).astype(BF16)

    def cumsum_rows(v):
        r = bdot(tri_lo, jnp.concatenate(split3(v), axis=1))
        n = v.shape[1]
        return r[:, :n] + r[:, n:2 * n] + r[:, 2 * n:]

    def expand_heads(pieces, cols):
        r = bdot(jnp.concatenate(pieces, axis=0), expand[:, cols])
        acc = r[0:q]
        for i in range(1, len(pieces)):
            acc = acc + r[i * q:(i + 1) * q]
        return acc

    def fwd(c, carry):
        r0 = pl.multiple_of(c * q, q)
        dt = _softplus(dt_ref[pl.ds(r0, q), :] + dtb_ref[...])
        adt = dt * arow_ref[...]
        cs = cumsum_rows(adt)
        rcs = cs[q - 1:q, :] - cs + adt
        lane = lax.broadcasted_iota(jnp.int32, (q, LANES), 1)
        cs_t = jnp.where(lane < hpg, cs, rcs).T
        x = xc_ref[pl.ds(r0, q), :]
        bm = bc_ref[pl.ds(r0, q), :]
        cm = cc_ref[pl.ds(r0, q), :].astype(BF16)
        cb = lax.dot_general(cm, bm.astype(BF16), (((1,), (1,)), ((), ())), preferred_element_type=F32)
        dtx = expand_heads(split3(dt)[:2], slice(0, 2 * wid))
        csx = expand_heads(split3(cs), slice(0, wid))
        rcsx = expand_heads(split3(rcs), slice(wid, 2 * wid))
        xs_f = x * dtx[:, :wid]
        xs_b = x * dtx[:, wid:]
        xsb_ref[pl.ds(r0, q), :] = xs_b
        rcsx_ref[pl.ds(r0, q), :] = rcsx
        parts = []
        for r in range(hpg):
            cols = slice(r * p, (r + 1) * p)
            lf = jnp.exp(jnp.where(ii >= jj, cs[:, r:r + 1] - cs_t[r:r + 1, :], neg))
            lb = jnp.exp(jnp.where(jj >= ii, rcs[:, hpg + r:hpg + r + 1] - cs_t[hpg + r:hpg + r + 1, :], neg))
            lhs = jnp.concatenate([cb * lf, cb * lb], axis=1).astype(BF16)
            rhs = jnp.concatenate([xs_f[:, cols], xs_b[:, cols]], axis=0).astype(BF16)
            parts.append(bdot(lhs, rhs))
        y = jnp.concatenate(parts, axis=1)
        st = stf_ref[...]
        y = y + bdot(cm, st.astype(BF16)) * jnp.exp(csx) + drow_ref[...] * x
        y_ref[pl.ds(r0, q), :] = y
        tot = csx[q - 1:q, :]
        dec = jnp.exp(tot - csx)
        stf_ref[...] = st * jnp.exp(tot) + bdot(bm.T.astype(BF16), (xs_f * dec).astype(BF16))
        return carry

    def bwd(k_it, carry):
        c = nc - 1 - k_it
        r0 = pl.multiple_of(c * q, q)
        rcsx = rcsx_ref[pl.ds(r0, q), :]
        xs_b = xsb_ref[pl.ds(r0, q), :]
        bm = bc_ref[pl.ds(r0, q), :]
        cm = cc_ref[pl.ds(r0, q), :].astype(BF16)
        st = stb_ref[...]
        y_ref[pl.ds(r0, q), :] += bdot(cm, st.astype(BF16)) * jnp.exp(rcsx)
        tot = rcsx[0:1, :]
        dec = jnp.exp(tot - rcsx)
        stb_ref[...] = st * jnp.exp(tot) + bdot(bm.T.astype(BF16), (xs_b * dec).astype(BF16))
        return carry

    lax.fori_loop(0, nc, fwd, 0)
    lax.fori_loop(0, nc, bwd, 0)
    if want_fin:
        fin_f, fin_b = stf_ref[...].T, stb_ref[...].T
        for r in range(hpg):
            fin_ref[0, r] = fin_f[r * p:(r + 1) * p, :]
            fin_ref[1, r] = fin_b[r * p:(r + 1) * p, :]


def _ssd_scan(proj, dtg, conv_w, conv_b, dtb_rows, a_rows, d_rows, init, y_prev, fin_prev, *, layer, row0, n_b,
              seq, seg, want_fin):
    q, p, hpg, ng, ns = SSD_CHUNK, SSD_HEADDIM, SSD_HPG, SSD_NGROUPS, SSD_DSTATE
    wid = hpg * p
    rb0 = row0 // seq
    xcol0 = OFF_XBC // wid
    bcol0 = (OFF_XBC + D_SSD) // ns
    ccol0 = bcol0 + ng
    in_specs = [
        pl.BlockSpec((seq, wid), lambda b, g: (rb0 + b, xcol0 + g)),
        pl.BlockSpec((seq, ns), lambda b, g: (rb0 + b, bcol0 + g)),
        pl.BlockSpec((seq, ns), lambda b, g: (rb0 + b, ccol0 + g)),
        pl.BlockSpec((None, seq, LANES), lambda b, g: (g, rb0 + b, 0)),
        pl.BlockSpec((3, wid), lambda b, g: (0, g)),
        pl.BlockSpec((3, ns), lambda b, g: (0, D_SSD // ns + g)),
        pl.BlockSpec((3, ns), lambda b, g: (0, D_SSD // ns + ng + g)),
        pl.BlockSpec((1, wid), lambda b, g: (0, g)),
        pl.BlockSpec((1, ns), lambda b, g: (0, D_SSD // ns + g)),
        pl.BlockSpec((1, ns), lambda b, g: (0, D_SSD // ns + ng + g)),
        pl.BlockSpec((None, 1, LANES), lambda b, g: (g, 0, 0)),
        pl.BlockSpec((None, 1, LANES), lambda b, g: (g, 0, 0)),
        pl.BlockSpec((None, 1, wid), lambda b, g: (g, 0, 0)),
    ]
    args = [proj, proj, proj, dtg, conv_w, conv_w, conv_w, conv_b, conv_b, conv_b, dtb_rows, a_rows, d_rows]
    if init is not None:
        in_specs.append(pl.BlockSpec((None, 2, hpg, p, ns), lambda b, g: (b, 0, g, 0, 0)))
        args.append(init)
    aliases = {}
    for k_out, prev in enumerate((y_prev, fin_prev)):
        if prev is not None:
            in_specs.append(pl.BlockSpec(memory_space=pl.ANY))
            args.append(prev)
            aliases[len(args) - 1] = k_out
    out_specs = [pl.BlockSpec((seq, wid), lambda b, g: (rb0 + b, g))]
    out_shape = [jax.ShapeDtypeStruct((proj.shape[0], D_SSD), F32)]
    if want_fin:
        out_specs.append(pl.BlockSpec((None, None, 2, hpg, p, ns), lambda b, g: (b, layer, 0, g, 0, 0)))
        out_shape.append(jax.ShapeDtypeStruct((n_b, DEPTH, 2, SSD_HEADS, p, ns), F32))
    else:
        assert fin_prev is None
    res = pl.pallas_call(
        functools.partial(_ssd_body, seq=seq, seg=seg, has_init=init is not None, want_fin=want_fin,
                          n_aliased=len(aliases)),
        grid=(n_b, ng),
        in_specs=in_specs,
        out_specs=out_specs,
        out_shape=out_shape,
        input_output_aliases=aliases,
        scratch_shapes=[pltpu.VMEM((seq, wid), F32), pltpu.VMEM((seq, ns), F32), pltpu.VMEM((seq, ns), F32),
                        pltpu.VMEM((ns, wid), F32), pltpu.VMEM((ns, wid), F32),
                        pltpu.VMEM((seq, wid), F32), pltpu.VMEM((seq, wid), F32)],
        compiler_params=_cparams(("arbitrary", "arbitrary")),
        name="ssd_scan",
    )(*args)
    return res if want_fin else (res[0], None)


def _ssd_norm_body(y_ref, z0_ref, z1_ref, w_ref, o_ref):
    z = jnp.concatenate([z0_ref[...], z1_ref[...]], axis=1)
    y = y_ref[...] * jax.nn.silu(z)
    ms = jnp.mean(y * y, axis=-1, keepdims=True)
    o_ref[...] = (y * lax.rsqrt(ms + EPS) * w_ref[...]).astype(o_ref.dtype)


def _ssd_norm(y, proj, norm_w, tm=256):
    m, d = y.shape
    half = d // 2
    zb = OFF_Z // half
    return pl.pallas_call(
        _ssd_norm_body,
        grid=(m // tm,),
        in_specs=[pl.BlockSpec((tm, d), lambda i: (i, 0)),
                  pl.BlockSpec((tm, half), lambda i: (i, zb)),
                  pl.BlockSpec((tm, half), lambda i: (i, zb + 1)),
                  pl.BlockSpec((1, d), lambda i: (0, 0))],
        out_specs=pl.BlockSpec((tm, d), lambda i: (i, 0)),
        out_shape=jax.ShapeDtypeStruct((m, d), BF16),
        compiler_params=_cparams(("arbitrary",)),
        name="ssd_norm",
    )(y, proj, proj, norm_w)


def _hyena_filters(seq, w1, b1, w2, b2, w3, freq):
    t = jnp.arange(seq, dtype=F32) / seq
    bands = jnp.linspace(1e-4, HY_BANDS - 1, HY_BANDS, dtype=F32)
    ang = 2.0 * math.pi * t[:, None] * bands[None, :]
    feat = jnp.concatenate([t[:, None], jnp.cos(ang), -jnp.sin(ang)], axis=-1)
    dot = functools.partial(jnp.dot, precision=HIGHEST)
    hdn = jnp.sin(freq[0] * (dot(feat, w1) + b1))
    hdn = jnp.sin(freq[1] * (dot(hdn, w2) + b2))
    filt = dot(hdn, w3).reshape(seq, 2, HY_ORDER, D_HY)
    deltas = jnp.abs(jnp.linspace(HY_MIN_DECAY, HY_MAX_DECAY, D_HY, dtype=F32))
    filt = filt * jnp.exp(-t[:, None] * deltas[None, :])[:, None, None, :]
    hf = filt[:, 0].reshape(seq, HY_ORDER * D_HY)
    hb = filt[:, 1].reshape(seq, HY_ORDER * D_HY)
    return hf + hb, hf - hb


def _split_bf16(x):
    hi = x.astype(BF16)
    return hi, (x - hi.astype(F32)).astype(BF16)


def _dft_ops(seq):
    n = 2 * seq
    f = jnp.arange(seq, dtype=jnp.int32)[:, None]
    t = jnp.arange(seq, dtype=jnp.int32)[None, :]
    ang = ((f * t) % n).astype(F32) * (math.pi / seq)
    cos_ft, sin_ft = jnp.cos(ang), jnp.sin(ang)
    nyq = jnp.where(t % 2 == 0, 1.0, -1.0).astype(F32)
    fwd = jnp.concatenate([cos_ft, nyq, -sin_ft[1:]], axis=0)
    wgt = jnp.where(f == 0, 1.0, 2.0).astype(F32) / n
    inv = jnp.concatenate([(wgt * cos_ft).T, nyq.T / n, (-(2.0 / n) * sin_ft[1:]).T], axis=1)
    return _split_bf16(fwd), _split_bf16(inv)


def _dot3(op_hi, op_lo, x):
    d = functools.partial(jnp.dot, preferred_element_type=F32)
    x_hi, x_lo = _split_bf16(x)
    return d(op_hi, x_hi) + (d(op_hi, x_lo) + d(op_lo, x_hi))


def _spectra_body(hs_ref, hd_ref, fh_ref, fl_ref, kr_ref, ki_ref, kn_ref, *, seq):
    fs = _dot3(fh_ref[...], fl_ref[...], hs_ref[...])
    fd = _dot3(fh_ref[...], fl_ref[...], hd_ref[...])
    row0 = lax.broadcasted_iota(jnp.int32, (seq, 1), 0) == 0
    kr_ref[...] = fs[:seq]
    kn_ref[...] = jnp.broadcast_to(fs[seq:seq + 1], kn_ref.shape)
    ki_ref[...] = jnp.where(row0, 0.0, fd[seq:])


def _hyena_spectra(hsum, hdiff, dft, seq, tc=256):
    n = hsum.shape[1]
    blk = pl.BlockSpec((seq, tc), lambda cb: (0, cb))
    const = pl.BlockSpec((2 * seq, seq), lambda cb: (0, 0), pipeline_mode=pl.Buffered(1))
    return pl.pallas_call(
        functools.partial(_spectra_body, seq=seq),
        grid=(n // tc,),
        in_specs=[blk, blk, const, const],
        out_specs=[blk, blk, pl.BlockSpec((8, tc), lambda cb: (0, cb))],
        out_shape=[jax.ShapeDtypeStruct((seq, n), F32)] * 2 + [jax.ShapeDtypeStruct((8, n), F32)],
        compiler_params=_cparams(("arbitrary",)),
        name="hyena_spectra",
    )(hsum, hdiff, *dft[0])


def _hyena_body(*refs, seq, seg, aliased):
    (v_ref, x1_ref, x2_ref, cwv_ref, cw1_ref, cw2_ref, cbv_ref, cb1_ref, cb2_ref,
     kr0_ref, kr1_ref, ki0_ref, ki1_ref, kn0_ref, kn1_ref, bias_ref, fh_ref, fl_ref, gh_ref, gl_ref) = refs[:20]
    o_ref = refs[-1]
    kr, ki, kn = (kr0_ref, kr1_ref), (ki0_ref, ki1_ref), (kn0_ref, kn1_ref)
    z = _conv3(v_ref[...], cwv_ref, cbv_ref, seg)
    gates = (_conv3(x1_ref[...], cw1_ref, cb1_ref, seg), _conv3(x2_ref[...], cw2_ref, cb2_ref, seg))
    row0 = lax.broadcasted_iota(jnp.int32, (seq, 1), 0) == 0
    for o in range(HY_ORDER):
        zf = _dot3(fh_ref[...], fl_ref[...], z)
        a, b = zf[:seq], zf[seq:]
        pre = a * kr[o][...] - b * ki[o][...]
        pim = a * ki[o][...] + b * jnp.where(row0, kn[o][0:1, :], kr[o][...])
        conv = _dot3(gh_ref[...], gl_ref[...], jnp.concatenate([pre, pim], axis=0))
        z = gates[o] * (conv + bias_ref[o:o + 1, :] * z)
    o_ref[...] = z.astype(o_ref.dtype)


def _hyena(proj, conv_w, conv_b, spectra, bias, dft, out_prev, *, n_rows, row0, n_b, seq, seg, tc):
    assert HY_ORDER == 2
    kr, ki, kn = spectra
    (fh, fl), (gh, gl) = dft
    rb0 = row0 // seq
    ncb = D_HY // tc
    assert PROJ_HY % tc == 0
    u_spec = lambda part: pl.BlockSpec((seq, tc), lambda cb, b: (rb0 + b, PROJ_HY // tc + part * ncb + cb))
    w_spec = lambda part: pl.BlockSpec((3, tc), lambda cb, b: (0, part * ncb + cb))
    b_spec = lambda part: pl.BlockSpec((1, tc), lambda cb, b: (0, part * ncb + cb))
    k_spec = lambda o: pl.BlockSpec((seq, tc), lambda cb, b: (0, o * ncb + cb), pipeline_mode=pl.Buffered(1))
    n_spec = lambda o: pl.BlockSpec((8, tc), lambda cb, b: (0, o * ncb + cb))
    const = lambda shape: pl.BlockSpec(shape, lambda cb, b: (0, 0), pipeline_mode=pl.Buffered(1))
    in_specs = [u_spec(0), u_spec(1), u_spec(2), w_spec(0), w_spec(1), w_spec(2), b_spec(0), b_spec(1), b_spec(2),
                k_spec(0), k_spec(1), k_spec(0), k_spec(1), n_spec(0), n_spec(1),
                pl.BlockSpec((HY_ORDER, tc), lambda cb, b: (0, cb)),
                const((2 * seq, seq)), const((2 * seq, seq)), const((seq, 2 * seq)), const((seq, 2 * seq))]
    args = [proj, proj, proj, conv_w, conv_w, conv_w, conv_b, conv_b, conv_b, kr, kr, ki, ki, kn, kn, bias,
            fh, fl, gh, gl]
    aliases = {}
    if out_prev is not None:
        in_specs.append(pl.BlockSpec(memory_space=pl.ANY))
        args.append(out_prev)
        aliases = {len(args) - 1: 0}
    return pl.pallas_call(
        functools.partial(_hyena_body, seq=seq, seg=seg, aliased=out_prev is not None),
        grid=(ncb, n_b),
        in_specs=in_specs,
        out_specs=pl.BlockSpec((seq, tc), lambda cb, b: (rb0 + b, cb)),
        out_shape=jax.ShapeDtypeStruct((n_rows, D_HY), BF16),
        input_output_aliases=aliases,
        compiler_params=_cparams(("arbitrary", "arbitrary")),
        name="hyena",
    )(*args)


def _s5_mixer(proj, p, ops, l, dims):
    n_cb, n_cs, n_lb, n_ls = dims
    t_ctx = n_cb * n_cs
    nq_c, nq_l = n_cs // S5_Q, n_ls // S5_Q
    u = lax.optimization_barrier(proj[:, :D_S5]).astype(BF16)

    def to_rows(a, nb, nq):
        a = jnp.transpose(a.reshape(nb, nq, S5_Q, S5_GROUPS, S5_H), (3, 1, 0, 2, 4))
        return a.reshape(S5_GROUPS, nq * nb, S5_Q * S5_H)

    def from_rows(a, nb, nq):
        a = jnp.transpose(a.reshape(S5_GROUPS, nq, nb, S5_Q, S5_H), (2, 1, 3, 0, 4))
        return a.reshape(nb * nq * S5_Q, D_S5)

    u_rows = jnp.concatenate([to_rows(u[:t_ctx], n_cb, nq_c), to_rows(u[t_ctx:], n_lb, nq_l)], axis=1)
    sre, sim = p['state_s5_re'][:, l], p['state_s5_im'][:, l]
    x0 = jnp.transpose(jnp.concatenate([sre[:, 0], sre[:, 1], sim[:, 0], sim[:, 1]], axis=-1), (1, 0, 2))
    y_rows, fin = _s5_scan(u_rows, ops, x0, n_cb, nq_c, n_lb, nq_l)
    y = jnp.concatenate([from_rows(y_rows[:, :nq_c * n_cb], n_cb, nq_c),
                         from_rows(y_rows[:, nq_c * n_cb:], n_lb, nq_l)], axis=0)
    y = _s5_glu(y, p['s5_w_glu'], l)
    fin = jnp.transpose(fin.reshape(S5_GROUPS, n_cb, 4, S5_P), (1, 2, 0, 3))
    return y, fin[:, 0:2], fin[:, 2:4]


def _ssd_mixer(proj, dt_raw, p, l, dims, fin_prev):
    n_cb, n_cs, n_lb, n_ls = dims
    t_ctx = n_cb * n_cs
    t_all = proj.shape[0]
    pad_heads = jnp.zeros((SSD_NGROUPS, LANES - 2 * SSD_HPG), F32)

    def per_group(a):
        return jnp.concatenate([a[0].reshape(SSD_NGROUPS, SSD_HPG), a[1].reshape(SSD_NGROUPS, SSD_HPG),
                                pad_heads], axis=1)[:, None, :]

    dtb_rows = per_group(p['ssd_dt_bias'][l])
    a_rows = per_group(-jnp.exp(p['ssd_a_log'][l]))
    d_rows = jnp.repeat(p['ssd_d'][l], SSD_HEADDIM).reshape(SSD_NGROUPS, 1, SSD_HPG * SSD_HEADDIM)
    dtr = dt_raw.reshape(t_all, 2, SSD_NGROUPS, SSD_HPG)
    dtg = jnp.concatenate([jnp.transpose(dtr, (2, 0, 1, 3)).reshape(SSD_NGROUPS, t_all, 2 * SSD_HPG),
                           jnp.zeros((SSD_NGROUPS, t_all, LANES - 2 * SSD_HPG), F32)], axis=-1)
    cw, cb = p['ssd_conv_w'][l], p['ssd_conv_b'][l][None]
    y, fin = _ssd_scan(proj, dtg, cw, cb, dtb_rows, a_rows, d_rows, None, None, fin_prev,
                       layer=l, row0=0, n_b=n_cb, seq=n_cs, seg=n_cs, want_fin=True)
    y, _ = _ssd_scan(proj, dtg, cw, cb, dtb_rows, a_rows, d_rows, p['state_ssd'][:, l], y, None,
                     layer=l, row0=t_ctx, n_b=n_lb, seq=n_ls, seg=GRID_W, want_fin=False)
    return _ssd_norm(y, proj, p['ssd_norm'][l][None]), fin


def _hyena_mixer(proj, p, l, dims, dfts):
    n_cb, n_cs, n_lb, n_ls = dims
    hcw, hcb = p['hy_conv_w'][l], p['hy_conv_b'][l][None]
    out = None
    cfgs = ((0, n_cb, n_cs, n_cs, 512), (n_cb * n_cs, n_lb, n_ls, GRID_W, 256))
    for (row0, nb, seq, seg, tc), dft in zip(cfgs, dfts):
        hsum, hdiff = _hyena_filters(seq, p['hy_w1'][l], p['hy_b1'][l], p['hy_w2'][l], p['hy_b2'][l], p['hy_w3'][l],
                                     p['hy_freq'][l])
        out = _hyena(proj, hcw, hcb, _hyena_spectra(hsum, hdiff, dft, seq), p['hy_bias'][l], dft, out,
                     n_rows=proj.shape[0], row0=row0, n_b=nb, seq=seq, seg=seg, tc=tc)
    return out


def _in_proj_body(a_ref, wt_ref, o_ref, wbf_ref):
    @pl.when(pl.program_id(1) == 0)
    def _():
        wbf_ref[...] = wt_ref[...].astype(BF16)

    o_ref[...] = lax.dot_general(a_ref[...], wbf_ref[...], (((1,), (1,)), ((), ())), preferred_element_type=F32)


def _in_proj(h, w_in, l, tm=1024):
    m, d = h.shape
    assert N_IN % 8 == 0 and OFF_HY % 8 == 0
    wt = jnp.swapaxes(w_in, 1, 2).reshape(w_in.shape[0] * N_IN, d)
    n_main, n_hy = OFF_DT // PROJ_TN, (N_IN - OFF_HY) // PROJ_TN
    assert OFF_DT % PROJ_TN == 0 and (N_IN - OFF_HY) % PROJ_TN == 0 and OFF_DT + PROJ_TN <= N_IN
    assert (n_main + n_hy + 1) * PROJ_TN == N_PROJ and m % tm == 0

    def w_row(j):
        return jnp.where(j < n_main, j * PROJ_TN, jnp.where(j < n_main + n_hy, OFF_HY + (j - n_main) * PROJ_TN, OFF_DT))

    return pl.pallas_call(
        _in_proj_body,
        grid=(N_PROJ // PROJ_TN, m // tm),
        in_specs=[pl.BlockSpec((tm, d), lambda j, i: (i, 0)),
                  pl.BlockSpec((pl.Element(PROJ_TN), pl.Element(d)),
                               lambda j, i: (pl.multiple_of(l * N_IN + w_row(j), 8), 0))],
        out_specs=pl.BlockSpec((tm, PROJ_TN), lambda j, i: (i, j)),
        out_shape=jax.ShapeDtypeStruct((m, N_PROJ), F32),
        scratch_shapes=[pltpu.VMEM((PROJ_TN, d), BF16)],
        compiler_params=_cparams(("arbitrary", "arbitrary")),
        name="w_in",
    )(h, wt)


def _moe(h_packed, route, wg, wu, wd, lead):
    t_all = h_packed.shape[0]
    token_of_row, row_of_pair, item_e, item_b, item_rows = _moe_dispatch(route, t_all)
    xs = jnp.take(h_packed, token_of_row, axis=0, mode="clip")
    ys = _ffn(xs, wg, wu, wd, lead, item_e, item_b, item_rows, packed=True, row_variants=FFN_MOE_ROWS)
    return (jnp.take(ys, row_of_pair[:, 0], axis=0, mode="clip"),
            jnp.take(ys, row_of_pair[:, 1], axis=0, mode="clip"))


def kernel(x_prompt, x_sample, state_s5_re, state_s5_im, state_ssd, c, c_ctx, ada_w, ada_b, norm1, norm2, final_norm, w_in, w_out, s5_a_re, s5_a_im, s5_log_dt, s5_b_re, s5_b_im, s5_c_re, s5_c_im, s5_d, s5_w_glu, ssd_conv_w, ssd_conv_b, ssd_dt_bias, ssd_a_log, ssd_d, ssd_norm, hy_conv_w, hy_conv_b, hy_w1, hy_b1, hy_w2, hy_b2, hy_w3, hy_freq, hy_bias, ffn_wg, ffn_wu, ffn_wd, moe_router, moe_wg, moe_wu, moe_wd):
    n_cb, n_cs, d = x_prompt.shape
    n_lb, n_ls, _ = x_sample.shape
    assert d == D_MODEL and n_ls == LAT_ROWS and n_cs % SSD_CHUNK == 0 and n_ls % SSD_CHUNK == 0
    t_ctx, t_lat = n_cb * n_cs, n_lb * n_ls
    t_all = t_ctx + t_lat
    assert t_ctx % LAT_ROWS == 0 and t_all % FFN_ROWS == 0 and 1 + n_lb <= 3
    x = jnp.concatenate([x_prompt.reshape(t_ctx, d), x_sample.reshape(t_lat, d)], axis=0)

    cond = jnp.zeros((8, d), F32).at[0].set(c_ctx).at[1:1 + n_lb].set(c)
    mod = jnp.concatenate(
        [_matmul(cond, ada_w, l, 6 * d, tm=8, tn=512, out_dtype=F32, a_silu=True,
                 bias=ada_b.reshape(DEPTH, 1, 6 * d), name="ada_mod")[None] for l in range(DEPTH)], axis=0)
    mod3 = mod[:, :3].reshape(DEPTH, 3, 1, 6 * d)

    p = dict(state_s5_re=state_s5_re, state_s5_im=state_s5_im, state_ssd=state_ssd,
             s5_a_re=s5_a_re, s5_a_im=s5_a_im, s5_log_dt=s5_log_dt, s5_b_re=s5_b_re, s5_b_im=s5_b_im,
             s5_c_re=s5_c_re, s5_c_im=s5_c_im, s5_d=s5_d, s5_w_glu=s5_w_glu,
             ssd_conv_w=ssd_conv_w, ssd_conv_b=ssd_conv_b, ssd_dt_bias=ssd_dt_bias, ssd_a_log=ssd_a_log,
             ssd_d=ssd_d, ssd_norm=ssd_norm, hy_conv_w=hy_conv_w, hy_conv_b=hy_conv_b, hy_w1=hy_w1, hy_b1=hy_b1,
             hy_w2=hy_w2, hy_b2=hy_b2, hy_w3=hy_w3, hy_freq=hy_freq, hy_bias=hy_bias)
    dims = (n_cb, n_cs, n_lb, n_ls)
    dfts = (_dft_ops(n_cs), _dft_ops(n_ls))
    s5_ops = jax.vmap(_s5_prep)(s5_a_re, s5_a_im, s5_log_dt, s5_b_re, s5_b_im, s5_c_re, s5_c_im, s5_d)
    fin_re, fin_im, fin_ssd = [], [], None
    for l in range(DEPTH):
        m3 = mod3[l]
        h = _norm_mod(x, norm1[l][None], m3, 1, 0, t_ctx)
        proj = _in_proj(h, w_in, l)
        dt_raw = proj[:, PROJ_DT:PROJ_DT + 2 * SSD_HEADS]
        y_s5, f_re, f_im = _s5_mixer(proj, p, tuple(o[l] for o in s5_ops), l, dims)
        y_ssd, fin_ssd = _ssd_mixer(proj, dt_raw, p, l, dims, fin_ssd)
        y_hy = _hyena_mixer(proj, p, l, dims, dfts)
        fin_re.append(f_re)
        fin_im.append(f_im)
        x = _matmul([y_s5, y_ssd, y_hy], w_out, l, d, tm=1024, tn=512, out_dtype=F32, resid=x,
                    gate=m3[:, :, 2 * d:3 * d], n_ctx_rows=t_ctx, name="w_out")
        final_w = final_norm[None] if l == DEPTH - 1 else None
        if l % 2 == 0:
            h = _norm_mod(x, norm2[l][None], m3, 4, 3, t_ctx)
            n_items = t_all // FFN_ROWS
            f = _ffn(h, ffn_wg, ffn_wu, ffn_wd, l // 2, jnp.zeros((n_items,), jnp.int32),
                     jnp.arange(n_items, dtype=jnp.int32), jnp.full((n_items,), FFN_ROWS, jnp.int32),
                     packed=False, row_variants=FFN_DENSE_ROWS)
            x = _resid(x, [f], m3, 5, t_ctx, final_w=final_w)
        else:
            router = jnp.concatenate([moe_router[l // 2], jnp.zeros((d, LANES - N_EXPERTS), F32)], axis=1)
            h_packed, route = _norm_mod(x, norm2[l][None], m3, 4, 3, t_ctx, router=router)
            x = _resid(x, _moe(h_packed, route, moe_wg, moe_wu, moe_wd, l // 2), m3, 5, t_ctx, route=route,
                       final_w=final_w)

    y_prompt = x[:t_ctx].reshape(n_cb, n_cs, d)
    y_sample = x[t_ctx:].reshape(n_lb, n_ls, d)
    return (y_prompt, y_sample, jnp.stack(fin_re, axis=1), jnp.stack(fin_im, axis=1), fin_ssd)
```

```python
import functools
import math

import jax
import jax.numpy as jnp
from jax import lax
from jax.experimental import pallas as pl
from jax.experimental.pallas import tpu as pltpu

F32 = jnp.float32
BF16 = jnp.bfloat16
HIGHEST = lax.Precision.HIGHEST

D_MODEL = 4096
DEPTH = 2
GRID_W = 64
EPS = 1e-6
D_S5 = D_MODEL // 4
D_SSD = D_MODEL // 2
D_HY = D_MODEL - D_S5 - D_SSD
S5_H = 16
S5_GROUPS = D_S5 // S5_H
S5_P = 64
S5_Q = 16
SSD_HEADDIM = 64
SSD_HEADS = D_SSD // SSD_HEADDIM
SSD_NGROUPS = 4
SSD_HPG = SSD_HEADS // SSD_NGROUPS
SSD_DSTATE = 128
SSD_CHUNK = 128
SSD_CONV_DIM = D_SSD + 2 * SSD_NGROUPS * SSD_DSTATE
HY_ORDER = 2
HY_BANDS = 16
HY_MIN_DECAY = math.log(1e-2) / 1.5
HY_MAX_DECAY = math.log(1e-2) / 0.3
D_FF = 14336
N_EXPERTS = 8
OFF_Z = D_S5
OFF_XBC = OFF_Z + D_SSD
OFF_DT = OFF_XBC + SSD_CONV_DIM
OFF_HY = OFF_DT + 2 * SSD_HEADS
N_IN = OFF_HY + (HY_ORDER + 1) * D_HY

LANES = 128
VMEM_LIMIT = 56 * 1024 * 1024
LAT_ROWS = 1024
PROJ_TN = 512
N_PROJ = -(-N_IN // PROJ_TN) * PROJ_TN
PROJ_HY = OFF_DT
PROJ_DT = PROJ_HY + (HY_ORDER + 1) * D_HY


def _cparams(sem, vmem=VMEM_LIMIT):
    return pltpu.CompilerParams(dimension_semantics=sem, vmem_limit_bytes=vmem)


def _mod_row(i, tm, n_ctx_rows):
    return jnp.maximum((i * tm - n_ctx_rows) // LAT_ROWS + 1, 0)


def _mm_body(*refs, n_a, a_silu, has_bias, has_resid):
    a_refs, w_ref = refs[:n_a], refs[n_a]
    k = n_a + 1
    bias_ref = resid_ref = gate_ref = None
    if has_bias:
        bias_ref = refs[k]
        k += 1
    if has_resid:
        resid_ref, gate_ref = refs[k], refs[k + 1]
        k += 2
    o_ref, wbf_ref = refs[k], refs[k + 1]

    @pl.when(pl.program_id(1) == 0)
    def _():
        wbf_ref[...] = w_ref[...].astype(BF16)

    acc = None
    k0 = 0
    for a_ref in a_refs:
        a = a_ref[...]
        if a_silu:
            a = jax.nn.silu(a).astype(BF16)
        part = jnp.dot(a, wbf_ref[k0:k0 + a.shape[1], :], preferred_element_type=F32)
        acc = part if acc is None else acc + part
        k0 += a.shape[1]
    if has_bias:
        acc = acc + bias_ref[...]
    if has_resid:
        acc = resid_ref[...] + gate_ref[...] * acc
    o_ref[...] = acc.astype(o_ref.dtype)


def _matmul(a, w, w_lead, n_cols, *, tm, tn, out_dtype, a_silu=False, bias=None, resid=None, gate=None,
            n_ctx_rows=0, name="matmul"):
    a_list = list(a) if isinstance(a, (list, tuple)) else [a]
    m = a_list[0].shape[0]
    kdim = sum(x.shape[1] for x in a_list)
    assert m % tm == 0 and n_cols % tn == 0 and kdim == w.shape[-2]
    grid = (n_cols // tn, m // tm)
    w_spec = pl.BlockSpec((None, kdim, tn), lambda j, i: (w_lead, 0, j))
    in_specs = [pl.BlockSpec((tm, x.shape[1]), lambda j, i: (i, 0)) for x in a_list] + [w_spec]
    args = a_list + [w]
    if bias is not None:
        in_specs.append(pl.BlockSpec((None, 1, tn), lambda j, i: (w_lead, 0, j)))
        args.append(bias)
    if resid is not None:
        in_specs.append(pl.BlockSpec((tm, tn), lambda j, i: (i, j)))
        in_specs.append(pl.BlockSpec((None, 1, tn), lambda j, i: (_mod_row(i, tm, n_ctx_rows), 0, j)))
        args += [resid, gate]
    return pl.pallas_call(
        functools.partial(_mm_body, n_a=len(a_list), a_silu=a_silu, has_bias=bias is not None,
                          has_resid=resid is not None),
        grid=grid,
        in_specs=in_specs,
        out_specs=pl.BlockSpec((tm, tn), lambda j, i: (i, j)),
        out_shape=jax.ShapeDtypeStruct((m, n_cols), out_dtype),
        scratch_shapes=[pltpu.VMEM((kdim, tn), BF16)],
        compiler_params=_cparams(("arbitrary", "arbitrary")),
        name=name,
    )(*args)


def _norm_mod_body(x_ref, w_ref, sc_ref, sh_ref, *rest, with_router):
    x = x_ref[...]
    ms = jnp.mean(x * x, axis=-1, keepdims=True)
    y = x * lax.rsqrt(ms + EPS) * w_ref[...]
    h = y * (1.0 + sc_ref[...]) + sh_ref[...]
    if not with_router:
        (o_ref,) = rest
        o_ref[...] = h.astype(o_ref.dtype)
        return
    r_ref, o_ref, route_ref = rest
    half = h.shape[1] // 2
    hb = h.astype(BF16)
    lo = lax.bitcast_convert_type(hb[:, :half].astype(F32), jnp.uint32)
    hi = lax.bitcast_convert_type(hb[:, half:].astype(F32), jnp.uint32)
    o_ref[...] = lax.bitcast_convert_type((lo >> 16) | hi, F32)
    logits = jnp.dot(h, r_ref[...], preferred_element_type=F32, precision=HIGHEST)
    lane = lax.broadcasted_iota(jnp.int32, logits.shape, 1).astype(F32)
    neg = jnp.float32(-jnp.inf)
    lg = jnp.where(lane < N_EXPERTS, logits, neg)
    m1 = jnp.max(lg, axis=-1, keepdims=True)
    i1 = jnp.min(jnp.where(lg == m1, lane, float(LANES)), axis=-1, keepdims=True)
    lg2 = jnp.where(lane == i1, neg, lg)
    m2 = jnp.max(lg2, axis=-1, keepdims=True)
    i2 = jnp.min(jnp.where(lg2 == m2, lane, float(LANES)), axis=-1, keepdims=True)
    e = jnp.exp(m2 - m1)
    g1 = 1.0 / (1.0 + e)
    g2 = e / (1.0 + e)
    route = jnp.where(lane == 0, i1, jnp.where(lane == 1, i2, jnp.where(lane == 2, g1, jnp.where(lane == 3, g2, 0.0))))
    route_ref[...] = route


def _norm_mod(x, w_row, mod3, k_scale, k_shift, n_ctx_rows, router=None, tm=256):
    m, d = x.shape
    in_specs = [
        pl.BlockSpec((tm, d), lambda i: (i, 0)),
        pl.BlockSpec((1, d), lambda i: (0, 0)),
        pl.BlockSpec((None, 1, d), lambda i: (_mod_row(i, tm, n_ctx_rows), 0, k_scale)),
        pl.BlockSpec((None, 1, d), lambda i: (_mod_row(i, tm, n_ctx_rows), 0, k_shift)),
    ]
    args = [x, w_row, mod3, mod3]
    out_specs = pl.BlockSpec((tm, d), lambda i: (i, 0))
    out_shape = jax.ShapeDtypeStruct((m, d), BF16)
    if router is not None:
        in_specs.append(pl.BlockSpec((d, LANES), lambda i: (0, 0)))
        args.append(router)
        out_specs = [pl.BlockSpec((tm, d // 2), lambda i: (i, 0)), pl.BlockSpec((tm, LANES), lambda i: (i, 0))]
        out_shape = [jax.ShapeDtypeStruct((m, d // 2), F32), jax.ShapeDtypeStruct((m, LANES), F32)]
    return pl.pallas_call(
        functools.partial(_norm_mod_body, with_router=router is not None),
        grid=(m // tm,),
        in_specs=in_specs,
        out_specs=out_specs,
        out_shape=out_shape,
        compiler_params=_cparams(("arbitrary",)),
        name="norm_mod",
    )(*args)


def _resid_body(*refs, n_f, final):
    x_ref, f_refs, g_ref = refs[0], refs[1:1 + n_f], refs[1 + n_f]
    rest = refs[2 + n_f:]
    if n_f == 1:
        f = f_refs[0][...]
    else:
        route_ref, rest = rest[0], rest[1:]
        f = route_ref[:, 2:3] * f_refs[0][...] + route_ref[:, 3:4] * f_refs[1][...]
    y = x_ref[...] + g_ref[...] * f
    if final:
        w_ref, o_ref = rest
        ms = jnp.mean(y * y, axis=-1, keepdims=True)
        y = y * lax.rsqrt(ms + EPS) * w_ref[...]
    else:
        (o_ref,) = rest
    o_ref[...] = y


def _resid(x, fs, mod3, k_gate, n_ctx_rows, route=None, final_w=None, rows=None, tm=256):
    d = x.shape[1]
    row0, m = rows if rows is not None else (0, x.shape[0])
    assert row0 % tm == 0 and m % tm == 0
    t0 = row0 // tm
    tok = pl.BlockSpec((tm, d), lambda i: (t0 + i, 0))
    in_specs = [tok] + [tok] * len(fs) + [
        pl.BlockSpec((None, 1, d), lambda i: (_mod_row(t0 + i, tm, n_ctx_rows), 0, k_gate))]
    args = [x, *fs, mod3]
    if route is not None:
        in_specs.append(pl.BlockSpec((tm, LANES), lambda i: (t0 + i, 0)))
        args.append(route)
    if final_w is not None:
        in_specs.append(pl.BlockSpec((1, d), lambda i: (0, 0)))
        args.append(final_w)
    return pl.pallas_call(
        functools.partial(_resid_body, n_f=len(fs), final=final_w is not None),
        grid=(m // tm,),
        in_specs=in_specs,
        out_specs=pl.BlockSpec((tm, d), lambda i: (i, 0)),
        out_shape=jax.ShapeDtypeStruct((m, d), F32),
        compiler_params=_cparams(("arbitrary",)),
        name="resid",
    )(*args)


FFN_ROWS = 1024
FFN_MOE_BLOCK = 832
FFN_TF = 256
FFN_KC = 1024
FFN_NC = 512
FFN_VMEM_LIMIT = 62 * 1024 * 1024
FFN_DENSE_ROWS = (FFN_ROWS,)
FFN_MOE_ROWS = (256, 512, 768, FFN_MOE_BLOCK)


def _ffn_body(ie_ref, ib_ref, nv_ref, x_ref, wg_ref, wu_ref, wd_ref, o_ref, *, packed, row_variants):
    it = pl.program_id(0)
    f = pl.program_id(1)
    nv = nv_ref[it]
    d = wg_ref.shape[0]
    gran = row_variants[0]

    @pl.when(f == 0)
    def _():
        o_ref[...] = jnp.zeros_like(o_ref)

    def x_chunk(m, kc):
        if not packed:
            return x_ref[0:m, kc * FFN_KC:(kc + 1) * FFN_KC]
        per_half = (d // 2) // FFN_KC
        c0 = (kc % per_half) * FFN_KC
        w = lax.bitcast_convert_type(x_ref[0:m, c0:c0 + FFN_KC], jnp.uint32)
        bits = (w << 16) if kc < per_half else (w & jnp.uint32(0xFFFF0000))
        return lax.bitcast_convert_type(bits, F32).astype(BF16)

    def compute(m):
        g = u = None
        for kc in range(d // FFN_KC):
            ks = slice(kc * FFN_KC, (kc + 1) * FFN_KC)
            xk = x_chunk(m, kc)
            pg = jnp.dot(xk, wg_ref[ks, :].astype(BF16), preferred_element_type=F32)
            pu = jnp.dot(xk, wu_ref[ks, :].astype(BF16), preferred_element_type=F32)
            g = pg if g is None else g + pg
            u = pu if u is None else u + pu
        a = (jax.nn.silu(g) * u).astype(BF16)
        for nc in range(d // FFN_NC):
            ns = slice(nc * FFN_NC, (nc + 1) * FFN_NC)
            o_ref[0:m, ns] += jnp.dot(a, wd_ref[:, ns].astype(BF16), preferred_element_type=F32)

    n_gran = (nv + gran - 1) // gran
    for m in row_variants:
        @pl.when(n_gran == -(-m // gran))
        def _(m=m):
            compute(m)


def _ffn(xs, wg, wu, wd, lead, item_expert, item_block, item_rows, *, packed, row_variants):
    r = xs.shape[0]
    d = wg.shape[-2]
    n_items = item_expert.shape[0]
    dff = wg.shape[-1]
    n_f = dff // FFN_TF
    block_rows = row_variants[-1]
    assert all(m % row_variants[0] == 0 for m in row_variants[:-1]) and r % block_rows == 0

    def f_idx(it, f, nv):
        return jnp.where(nv[it] > 0, f, n_f - 1)

    if wg.ndim == 4:
        wgu_spec = pl.BlockSpec((None, None, d, FFN_TF), lambda it, f, ie, ib, nv: (lead, ie[it], 0, f_idx(it, f, nv)))
        wd_spec = pl.BlockSpec((None, None, FFN_TF, d), lambda it, f, ie, ib, nv: (lead, ie[it], f_idx(it, f, nv), 0))
    else:
        wgu_spec = pl.BlockSpec((None, d, FFN_TF), lambda it, f, ie, ib, nv: (lead, 0, f_idx(it, f, nv)))
        wd_spec = pl.BlockSpec((None, FFN_TF, d), lambda it, f, ie, ib, nv: (lead, f_idx(it, f, nv), 0))
    grid_spec = pltpu.PrefetchScalarGridSpec(
        num_scalar_prefetch=3,
        grid=(n_items, n_f),
        in_specs=[
            pl.BlockSpec((block_rows, xs.shape[1]), lambda it, f, ie, ib, nv: (ib[it], 0),
                         pipeline_mode=pl.Buffered(1)),
            wgu_spec, wgu_spec, wd_spec,
        ],
        out_specs=pl.BlockSpec((block_rows, d), lambda it, f, ie, ib, nv: (ib[it], 0), pipeline_mode=pl.Buffered(1)),
    )
    return pl.pallas_call(
        functools.partial(_ffn_body, packed=packed, row_variants=row_variants),
        grid_spec=grid_spec,
        out_shape=jax.ShapeDtypeStruct((r, d), F32),
        compiler_params=_cparams(("arbitrary", "arbitrary"), vmem=FFN_VMEM_LIMIT),
        name="ffn",
    )(item_expert, item_block, item_rows, xs, wg, wu, wd)


def _moe_dispatch(route, n_tokens, block):
    e_idx = route[:, :2].astype(jnp.int32)
    flat_e = e_idx.reshape(-1)
    n_pairs = flat_e.shape[0]
    onehot = (flat_e[:, None] == jnp.arange(N_EXPERTS)[None, :]).astype(jnp.int32)
    seen = jnp.cumsum(onehot, axis=0)
    counts = seen[-1]
    rank = jnp.sum(onehot * seen, axis=1) - 1
    blocks_per = (counts + block - 1) // block
    blk_end = jnp.cumsum(blocks_per)
    blk_start = blk_end - blocks_per
    dest = jnp.sum(onehot * blk_start[None, :], axis=1) * block + rank
    n_items = (n_pairs + N_EXPERTS * (block - 1)) // block
    n_rows = n_items * block
    token_of_row = jnp.zeros((n_rows,), jnp.int32).at[dest].set(jnp.arange(n_pairs, dtype=jnp.int32) // 2)
    row_of_pair = dest.reshape(n_tokens, 2)
    blk = jnp.arange(n_items, dtype=jnp.int32)
    n_real = blk_end[-1]
    blk_c = jnp.minimum(blk, n_real - 1)
    item_e = jnp.minimum(jnp.sum(blk_c[:, None] >= blk_end[None, :], axis=1), N_EXPERTS - 1).astype(jnp.int32)
    rows_left = counts[item_e] - (blk_c - blk_start[item_e]) * block
    item_rows = jnp.where(blk < n_real, jnp.clip(rows_left, 0, block), 0).astype(jnp.int32)
    return token_of_row, row_of_pair, item_e, blk, item_rows


def _s5_prep(a_re, a_im, log_dt, b_re, b_im, c_re, c_im, d):
    q, g, p, h = S5_Q, S5_GROUPS, S5_P, S5_H
    dt = jnp.exp(log_dt)[..., None]
    k = jnp.arange(q + 1, dtype=F32)
    mag = jnp.exp((a_re * dt)[..., None] * k)
    ang = (a_im * dt)[..., None] * k
    pw_re, pw_im = mag * jnp.cos(ang), mag * jnp.sin(ang)
    ab_re, ab_im = pw_re[..., 1], pw_im[..., 1]
    den = a_re * a_re + a_im * a_im
    q_re = ((ab_re - 1.0) * a_re + ab_im * a_im) / den
    q_im = (ab_im * a_re - (ab_re - 1.0) * a_im) / den
    bb_re = q_re[..., None] * b_re[None] - q_im[..., None] * b_im[None]
    bb_im = q_re[..., None] * b_im[None] + q_im[..., None] * b_re[None]
    c_re_t = jnp.swapaxes(c_re, 1, 2)[None, ..., None]
    c_im_t = jnp.swapaxes(c_im, 1, 2)[None, ..., None]
    cp_re = c_re_t * pw_re[:, :, :, None, :] - c_im_t * pw_im[:, :, :, None, :]
    cp_im = c_re_t * pw_im[:, :, :, None, :] + c_im_t * pw_re[:, :, :, None, :]
    cpx_re = jnp.swapaxes(cp_re, 3, 4)[..., None]
    cpx_im = jnp.swapaxes(cp_im, 3, 4)[..., None]
    bbx_re = bb_re[:, :, :, None, None, :]
    bbx_im = bb_im[:, :, :, None, None, :]
    kern = jnp.transpose(jnp.sum(cpx_re * bbx_re - cpx_im * bbx_im, axis=2), (0, 2, 1, 3, 4))
    ii = jnp.arange(q)[:, None]
    jj = jnp.arange(q)[None, :]
    lag = ii - jj
    kf = jnp.where((lag >= 0)[..., None, None, None], kern[0][jnp.clip(lag, 0, q)], 0.0)
    kb = jnp.where((lag <= 0)[..., None, None, None], kern[1][jnp.clip(-lag, 0, q)], 0.0)
    m_intra = jnp.transpose(kf + kb, (2, 1, 4, 0, 3)).reshape(g, q * h, q * h)
    pf_re, pf_im = pw_re[0][..., q - 1 - jnp.arange(q)], pw_im[0][..., q - 1 - jnp.arange(q)]
    pb_re, pb_im = pw_re[1][..., :q], pw_im[1][..., :q]

    def contrib(p_re, p_im, b_r, b_i):
        w_re = p_re[:, :, :, None] * b_r[:, :, None, :] - p_im[:, :, :, None] * b_i[:, :, None, :]
        w_im = p_re[:, :, :, None] * b_i[:, :, None, :] + p_im[:, :, :, None] * b_r[:, :, None, :]
        to_rows = lambda w: jnp.transpose(w, (0, 2, 3, 1)).reshape(g, q * h, p)
        return to_rows(w_re), to_rows(w_im)

    wsf_re, wsf_im = contrib(pf_re, pf_im, bb_re[0], bb_im[0])
    wsb_re, wsb_im = contrib(pb_re, pb_im, bb_re[1], bb_im[1])
    w_state = jnp.concatenate([wsf_re, wsb_re, wsf_im, wsb_im], axis=-1)
    to_cols = lambda w: jnp.transpose(w, (0, 1, 3, 2)).reshape(g, p, q * h)
    fi = jnp.arange(q) + 1
    bi = q - jnp.arange(q)
    w_off = jnp.concatenate([to_cols(cp_re[0][..., fi]), to_cols(cp_re[1][..., bi]),
                             to_cols(-cp_im[0][..., fi]), to_cols(-cp_im[1][..., bi])], axis=1)
    aq_re, aq_im = pw_re[..., q], pw_im[..., q]
    a1 = jnp.concatenate([aq_re[0], aq_re[1]], axis=-1)[:, None, :]
    a2 = jnp.concatenate([aq_im[0], aq_im[1]], axis=-1)[:, None, :]
    d_row = jnp.tile(d.reshape(g, 1, h), (1, 1, q))
    return m_intra.astype(BF16), w_state.astype(BF16), w_off.astype(BF16), a1, a2, d_row


def _s5_body(u_ref, mi_ref, ws_ref, wo_ref, a1_ref, a2_ref, d_ref, x0_ref, y_ref, fin_ref, con_ref, sp_ref,
             *, n_ctx_b, n_ctx_c, n_lat_b, n_lat_c):
    half = 2 * S5_P
    ub = u_ref[...]
    u = ub.astype(F32)
    con_ref[...] = jnp.dot(ub, ws_ref[...], preferred_element_type=F32)

    def scan(row0, nb, nc, s_re, s_im):
        ar = jnp.broadcast_to(a1_ref[...], (nb, half))
        ai = jnp.broadcast_to(a2_ref[...], (nb, half))
        is_fwd = lax.broadcasted_iota(jnp.int32, (nb, half), 1) < S5_P
        for k in range(nc):
            rf = slice(row0 + k * nb, row0 + (k + 1) * nb)
            rb = slice(row0 + (nc - 1 - k) * nb, row0 + (nc - k) * nb)
            sp_ref[rf, 0:S5_P] = s_re[:, :S5_P]
            sp_ref[rb, S5_P:half] = s_re[:, S5_P:]
            sp_ref[rf, half:half + S5_P] = s_im[:, :S5_P]
            sp_ref[rb, half + S5_P:2 * half] = s_im[:, S5_P:]
            c_re = jnp.where(is_fwd, con_ref[rf, 0:half], con_ref[rb, 0:half])
            c_im = jnp.where(is_fwd, con_ref[rf, half:2 * half], con_ref[rb, half:2 * half])
            s_re, s_im = ar * s_re - ai * s_im + c_re, ar * s_im + ai * s_re + c_im
        return s_re, s_im

    zeros = jnp.zeros((n_ctx_b, half), F32)
    fin_ref[:, 0:half], fin_ref[:, half:2 * half] = scan(0, n_ctx_b, n_ctx_c, zeros, zeros)
    scan(n_ctx_b * n_ctx_c, n_lat_b, n_lat_c, x0_ref[:, 0:half], x0_ref[:, half:2 * half])
    y = jnp.dot(ub, mi_ref[...], preferred_element_type=F32)
    y = y + jnp.dot(sp_ref[...].astype(BF16), wo_ref[...], preferred_element_type=F32)
    y_ref[...] = (y + u * d_ref[...]).astype(y_ref.dtype)


def _s5_scan(u_rows, ops, x0, n_ctx_b, n_ctx_c, n_lat_b, n_lat_c):
    g, r, w = u_rows.shape
    m_intra, w_state, w_off, a1, a2, d_row = ops
    mat = pl.BlockSpec((None, w, w), lambda i: (i, 0, 0))
    row = pl.BlockSpec((None, 1, w), lambda i: (i, 0, 0))
    return pl.pallas_call(
        functools.partial(_s5_body, n_ctx_b=n_ctx_b, n_ctx_c=n_ctx_c, n_lat_b=n_lat_b, n_lat_c=n_lat_c),
        grid=(g,),
        in_specs=[pl.BlockSpec((None, r, w), lambda i: (i, 0, 0)), mat, mat, mat,
                  pl.BlockSpec((None, 1, w // 2), lambda i: (i, 0, 0)),
                  pl.BlockSpec((None, 1, w // 2), lambda i: (i, 0, 0)), row,
                  pl.BlockSpec((None, n_lat_b, w), lambda i: (i, 0, 0))],
        out_specs=[pl.BlockSpec((None, r, w), lambda i: (i, 0, 0)),
                   pl.BlockSpec((None, n_ctx_b, w), lambda i: (i, 0, 0))],
        out_shape=[jax.ShapeDtypeStruct((g, r, w), BF16), jax.ShapeDtypeStruct((g, n_ctx_b, w), F32)],
        scratch_shapes=[pltpu.VMEM((r, w), F32), pltpu.VMEM((r, w), F32)],
        compiler_params=_cparams(("arbitrary",)),
        name="s5_scan",
    )(u_rows, m_intra, w_state, w_off, a1, a2, d_row, x0)


def _s5_glu_body(y_ref, w_ref, o_ref, wbf_ref):
    @pl.when(pl.program_id(0) == 0)
    def _():
        wbf_ref[...] = w_ref[...].astype(BF16)

    y = jax.nn.gelu(y_ref[...].astype(F32))
    z = jnp.dot(y.astype(BF16), wbf_ref[...], preferred_element_type=F32)
    o_ref[...] = (y * jax.nn.sigmoid(z)).astype(o_ref.dtype)


def _s5_glu(y, w_glu, lead, tm=512):
    m, d = y.shape
    return pl.pallas_call(
        _s5_glu_body,
        grid=(m // tm,),
        in_specs=[pl.BlockSpec((tm, d), lambda i: (i, 0)), pl.BlockSpec((None, d, d), lambda i: (lead, 0, 0))],
        out_specs=pl.BlockSpec((tm, d), lambda i: (i, 0)),
        out_shape=jax.ShapeDtypeStruct((m, d), BF16),
        scratch_shapes=[pltpu.VMEM((d, d), BF16)],
        compiler_params=_cparams(("arbitrary",)),
        name="s5_glu",
    )(y, w_glu)


def _conv3(u, w_ref, b_ref, seg):
    n = u.shape[0]
    assert seg & (seg - 1) == 0
    pos = lax.broadcasted_iota(jnp.int32, u.shape, 0) & (seg - 1)
    prev = jnp.where(pos == 0, 0.0, pltpu.roll(u, 1, 0))
    nxt = jnp.where(pos == seg - 1, 0.0, pltpu.roll(u, n - 1, 0))
    return b_ref[...] + w_ref[0:1, :] * prev + w_ref[1:2, :] * u + w_ref[2:3, :] * nxt


def _softplus(x):
    return jnp.maximum(x, 0.0) + jnp.log(1.0 + jnp.exp(-jnp.abs(x)))


def _ssd_body(*refs, seq, seg, has_init, want_fin, n_aliased):
    (x_ref, b_ref, c_ref, dt_ref, cwx_ref, cwb_ref, cwc_ref, cbx_ref, cbb_ref, cbc_ref,
     dtb_ref, arow_ref, drow_ref) = refs[:13]
    k = 13
    init_ref = None
    if has_init:
        init_ref = refs[k]
        k += 1
    k += n_aliased
    y_ref = refs[k]
    k += 1
    fin_ref = None
    if want_fin:
        fin_ref = refs[k]
        k += 1
    xc_ref, bc_ref, cc_ref, stf_ref, stb_ref, xsb_ref, rcsx_ref = refs[k:k + 7]
    q, p, hpg = SSD_CHUNK, SSD_HEADDIM, SSD_HPG
    wid = hpg * p
    nc = seq // q

    xc_ref[...] = jax.nn.silu(_conv3(x_ref[...], cwx_ref, cbx_ref, seg))
    bc_ref[...] = jax.nn.silu(_conv3(b_ref[...], cwb_ref, cbb_ref, seg))
    cc_ref[...] = jax.nn.silu(_conv3(c_ref[...], cwc_ref, cbc_ref, seg))
    if has_init:
        stf_ref[...] = jnp.concatenate([init_ref[0, r] for r in range(hpg)], axis=0).T
        stb_ref[...] = jnp.concatenate([init_ref[1, r] for r in range(hpg)], axis=0).T
    else:
        stf_ref[...] = jnp.zeros_like(stf_ref)
        stb_ref[...] = jnp.zeros_like(stb_ref)

    ii = lax.broadcasted_iota(jnp.int32, (q, q), 0)
    jj = lax.broadcasted_iota(jnp.int32, (q, q), 1)
    tri_lo = (ii >= jj).astype(BF16)
    er = lax.broadcasted_iota(jnp.int32, (LANES, 2 * wid), 0)
    ec = lax.broadcasted_iota(jnp.int32, (LANES, 2 * wid), 1) // p
    expand = (er == ec).astype(BF16)
    neg = jnp.float32(-jnp.inf)
    bdot = functools.partial(jnp.dot, preferred_element_type=F32)

    def split3(v):
        v1 = v.astype(BF16)
        r1 = v - v1.astype(F32)
        v2 = r1.astype(BF16)
        return v1, v2, (r1 - v2.astype(F32)).astype(BF16)

    def cumsum_rows(v):
        r = bdot(tri_lo, jnp.concatenate(split3(v), axis=1))
        n = v.shape[1]
        return r[:, :n] + r[:, n:2 * n] + r[:, 2 * n:]

    def expand_heads(pieces, cols):
        r = bdot(jnp.concatenate(pieces, axis=0), expand[:, cols])
        acc = r[0:q]
        for i in range(1, len(pieces)):
            acc = acc + r[i * q:(i + 1) * q]
        return acc

    def fwd(c, carry):
        r0 = pl.multiple_of(c * q, q)
        dt = _softplus(dt_ref[pl.ds(r0, q), :] + dtb_ref[...])
        adt = dt * arow_ref[...]
        cs = cumsum_rows(adt)
        rcs = cs[q - 1:q, :] - cs + adt
        lane = lax.broadcasted_iota(jnp.int32, (q, LANES), 1)
        cs_t = jnp.where(lane < hpg, cs, rcs).T
        x = xc_ref[pl.ds(r0, q), :]
        bm = bc_ref[pl.ds(r0, q), :]
        cm = cc_ref[pl.ds(r0, q), :].astype(BF16)
        cb = lax.dot_general(cm, bm.astype(BF16), (((1,), (1,)), ((), ())), preferred_element_type=F32)
        dtx = expand_heads(split3(dt)[:2], slice(0, 2 * wid))
        csx = expand_heads(split3(cs), slice(0, wid))
        rcsx = expand_heads(split3(rcs), slice(wid, 2 * wid))
        xs_f = x * dtx[:, :wid]
        xs_b = x * dtx[:, wid:]
        xsb_ref[pl.ds(r0, q), :] = xs_b
        rcsx_ref[pl.ds(r0, q), :] = rcsx
        parts = []
        for r in range(hpg):
            cols = slice(r * p, (r + 1) * p)
            lf = jnp.exp(jnp.where(ii >= jj, cs[:, r:r + 1] - cs_t[r:r + 1, :], neg))
            lb = jnp.exp(jnp.where(jj >= ii, rcs[:, hpg + r:hpg + r + 1] - cs_t[hpg + r:hpg + r + 1, :], neg))
            lhs = jnp.concatenate([cb * lf, cb * lb], axis=1).astype(BF16)
            rhs = jnp.concatenate([xs_f[:, cols], xs_b[:, cols]], axis=0).astype(BF16)
            parts.append(bdot(lhs, rhs))
        y = jnp.concatenate(parts, axis=1)
        st = stf_ref[...]
        y = y + bdot(cm, st.astype(BF16)) * jnp.exp(csx) + drow_ref[...] * x
        y_ref[pl.ds(r0, q), :] = y
        tot = csx[q - 1:q, :]
        dec = jnp.exp(tot - csx)
        stf_ref[...] = st * jnp.exp(tot) + bdot(bm.T.astype(BF16), (xs_f * dec).astype(BF16))
        return carry

    def bwd(k_it, carry):
        c = nc - 1 - k_it
        r0 = pl.multiple_of(c * q, q)
        rcsx = rcsx_ref[pl.ds(r0, q), :]
        xs_b = xsb_ref[pl.ds(r0, q), :]
        bm = bc_ref[pl.ds(r0, q), :]
        cm = cc_ref[pl.ds(r0, q), :].astype(BF16)
        st = stb_ref[...]
        y_ref[pl.ds(r0, q), :] += bdot(cm, st.astype(BF16)) * jnp.exp(rcsx)
        tot = rcsx[0:1, :]
        dec = jnp.exp(tot - rcsx)
        stb_ref[...] = st * jnp.exp(tot) + bdot(bm.T.astype(BF16), (xs_b * dec).astype(BF16))
        return carry

    lax.fori_loop(0, nc, fwd, 0)
    lax.fori_loop(0, nc, bwd, 0)
    if want_fin:
        fin_f, fin_b = stf_ref[...].T, stb_ref[...].T
        for r in range(hpg):
            fin_ref[0, r] = fin_f[r * p:(r + 1) * p, :]
            fin_ref[1, r] = fin_b[r * p:(r + 1) * p, :]


def _ssd_scan(proj, dtg, conv_w, conv_b, dtb_rows, a_rows, d_rows, init, y_prev, fin_prev, *, layer, row0, n_b,
              seq, seg, want_fin):
    q, p, hpg, ng, ns = SSD_CHUNK, SSD_HEADDIM, SSD_HPG, SSD_NGROUPS, SSD_DSTATE
    wid = hpg * p
    rb0 = row0 // seq
    xcol0 = OFF_XBC // wid
    bcol0 = (OFF_XBC + D_SSD) // ns
    ccol0 = bcol0 + ng
    in_specs = [
        pl.BlockSpec((seq, wid), lambda b, g: (rb0 + b, xcol0 + g)),
        pl.BlockSpec((seq, ns), lambda b, g: (rb0 + b, bcol0 + g)),
        pl.BlockSpec((seq, ns), lambda b, g: (rb0 + b, ccol0 + g)),
        pl.BlockSpec((None, seq, LANES), lambda b, g: (g, rb0 + b, 0)),
        pl.BlockSpec((3, wid), lambda b, g: (0, g)),
        pl.BlockSpec((3, ns), lambda b, g: (0, D_SSD // ns + g)),
        pl.BlockSpec((3, ns), lambda b, g: (0, D_SSD // ns + ng + g)),
        pl.BlockSpec((1, wid), lambda b, g: (0, g)),
        pl.BlockSpec((1, ns), lambda b, g: (0, D_SSD // ns + g)),
        pl.BlockSpec((1, ns), lambda b, g: (0, D_SSD // ns + ng + g)),
        pl.BlockSpec((None, 1, LANES), lambda b, g: (g, 0, 0)),
        pl.BlockSpec((None, 1, LANES), lambda b, g: (g, 0, 0)),
        pl.BlockSpec((None, 1, wid), lambda b, g: (g, 0, 0)),
    ]
    args = [proj, proj, proj, dtg, conv_w, conv_w, conv_w, conv_b, conv_b, conv_b, dtb_rows, a_rows, d_rows]
    if init is not None:
        in_specs.append(pl.BlockSpec((None, 2, hpg, p, ns), lambda b, g: (b, 0, g, 0, 0)))
        args.append(init)
    aliases = {}
    for k_out, prev in enumerate((y_prev, fin_prev)):
        if prev is not None:
            in_specs.append(pl.BlockSpec(memory_space=pl.ANY))
            args.append(prev)
            aliases[len(args) - 1] = k_out
    out_specs = [pl.BlockSpec((seq, wid), lambda b, g: (rb0 + b, g))]
    out_shape = [jax.ShapeDtypeStruct((proj.shape[0], D_SSD), F32)]
    if want_fin:
        out_specs.append(pl.BlockSpec((None, None, 2, hpg, p, ns), lambda b, g: (b, layer, 0, g, 0, 0)))
        out_shape.append(jax.ShapeDtypeStruct((n_b, DEPTH, 2, SSD_HEADS, p, ns), F32))
    else:
        assert fin_prev is None
    res = pl.pallas_call(
        functools.partial(_ssd_body, seq=seq, seg=seg, has_init=init is not None, want_fin=want_fin,
                          n_aliased=len(aliases)),
        grid=(n_b, ng),
        in_specs=in_specs,
        out_specs=out_specs,
        out_shape=out_shape,
        input_output_aliases=aliases,
        scratch_shapes=[pltpu.VMEM((seq, wid), F32), pltpu.VMEM((seq, ns), F32), pltpu.VMEM((seq, ns), F32),
                        pltpu.VMEM((ns, wid), F32), pltpu.VMEM((ns, wid), F32),
                        pltpu.VMEM((seq, wid), F32), pltpu.VMEM((seq, wid), F32)],
        compiler_params=_cparams(("arbitrary", "arbitrary")),
        name="ssd_scan",
    )(*args)
    return res if want_fin else (res[0], None)


def _ssd_norm_body(y_ref, z0_ref, z1_ref, w_ref, o_ref):
    z = jnp.concatenate([z0_ref[...], z1_ref[...]], axis=1)
    y = y_ref[...] * jax.nn.silu(z)
    ms = jnp.mean(y * y, axis=-1, keepdims=True)
    o_ref[...] = (y * lax.rsqrt(ms + EPS) * w_ref[...]).astype(o_ref.dtype)


def _ssd_norm(y, proj, norm_w, tm=256):
    m, d = y.shape
    half = d // 2
    zb = OFF_Z // half
    return pl.pallas_call(
        _ssd_norm_body,
        grid=(m // tm,),
        in_specs=[pl.BlockSpec((tm, d), lambda i: (i, 0)),
                  pl.BlockSpec((tm, half), lambda i: (i, zb)),
                  pl.BlockSpec((tm, half), lambda i: (i, zb + 1)),
                  pl.BlockSpec((1, d), lambda i: (0, 0))],
        out_specs=pl.BlockSpec((tm, d), lambda i: (i, 0)),
        out_shape=jax.ShapeDtypeStruct((m, d), BF16),
        compiler_params=_cparams(("arbitrary",)),
        name="ssd_norm",
    )(y, proj, proj, norm_w)


def _hyena_filters(seq, w1, b1, w2, b2, w3, freq):
    t = jnp.arange(seq, dtype=F32) / seq
    bands = jnp.linspace(1e-4, HY_BANDS - 1, HY_BANDS, dtype=F32)
    ang = 2.0 * math.pi * t[:, None] * bands[None, :]
    feat = jnp.concatenate([t[:, None], jnp.cos(ang), -jnp.sin(ang)], axis=-1)
    dot = functools.partial(jnp.dot, precision=HIGHEST)
    hdn = jnp.sin(freq[0] * (dot(feat, w1) + b1))
    hdn = jnp.sin(freq[1] * (dot(hdn, w2) + b2))
    filt = dot(hdn, w3).reshape(seq, 2, HY_ORDER, D_HY)
    deltas = jnp.abs(jnp.linspace(HY_MIN_DECAY, HY_MAX_DECAY, D_HY, dtype=F32))
    filt = filt * jnp.exp(-t[:, None] * deltas[None, :])[:, None, None, :]
    hf = filt[:, 0].reshape(seq, HY_ORDER * D_HY)
    hb = filt[:, 1].reshape(seq, HY_ORDER * D_HY)
    return hf + hb, hf - hb


def _split_bf16(x):
    hi = x.astype(BF16)
    return hi, (x - hi.astype(F32)).astype(BF16)


def _dft_ops(seq):
    n = 2 * seq
    f = jnp.arange(seq, dtype=jnp.int32)[:, None]
    t = jnp.arange(seq, dtype=jnp.int32)[None, :]
    ang = ((f * t) % n).astype(F32) * (math.pi / seq)
    cos_ft, sin_ft = jnp.cos(ang), jnp.sin(ang)
    nyq = jnp.where(t % 2 == 0, 1.0, -1.0).astype(F32)
    fwd = jnp.concatenate([cos_ft, nyq, -sin_ft[1:]], axis=0)
    wgt = jnp.where(f == 0, 1.0, 2.0).astype(F32) / n
    inv = jnp.concatenate([(wgt * cos_ft).T, nyq.T / n, (-(2.0 / n) * sin_ft[1:]).T], axis=1)
    return _split_bf16(fwd), _split_bf16(inv)


def _dot3(op_hi, op_lo, x):
    d = functools.partial(jnp.dot, preferred_element_type=F32)
    x_hi, x_lo = _split_bf16(x)
    return d(op_hi, x_hi) + (d(op_hi, x_lo) + d(op_lo, x_hi))


def _spectra_body(hs_ref, hd_ref, fh_ref, fl_ref, kr_ref, ki_ref, kn_ref, *, seq):
    fs = _dot3(fh_ref[...], fl_ref[...], hs_ref[...])
    fd = _dot3(fh_ref[...], fl_ref[...], hd_ref[...])
    row0 = lax.broadcasted_iota(jnp.int32, (seq, 1), 0) == 0
    kr_ref[...] = fs[:seq]
    kn_ref[...] = jnp.broadcast_to(fs[seq:seq + 1], kn_ref.shape)
    ki_ref[...] = jnp.where(row0, 0.0, fd[seq:])


def _hyena_spectra(hsum, hdiff, dft, seq, tc=256):
    n = hsum.shape[1]
    blk = pl.BlockSpec((seq, tc), lambda cb: (0, cb))
    const = pl.BlockSpec((2 * seq, seq), lambda cb: (0, 0), pipeline_mode=pl.Buffered(1))
    return pl.pallas_call(
        functools.partial(_spectra_body, seq=seq),
        grid=(n // tc,),
        in_specs=[blk, blk, const, const],
        out_specs=[blk, blk, pl.BlockSpec((8, tc), lambda cb: (0, cb))],
        out_shape=[jax.ShapeDtypeStruct((seq, n), F32)] * 2 + [jax.ShapeDtypeStruct((8, n), F32)],
        compiler_params=_cparams(("arbitrary",)),
        name="hyena_spectra",
    )(hsum, hdiff, *dft[0])


def _hyena_body(*refs, seq, seg, aliased):
    (v_ref, x1_ref, x2_ref, cwv_ref, cw1_ref, cw2_ref, cbv_ref, cb1_ref, cb2_ref,
     kr0_ref, kr1_ref, ki0_ref, ki1_ref, kn0_ref, kn1_ref, bias_ref, fh_ref, fl_ref, gh_ref, gl_ref) = refs[:20]
    o_ref = refs[-1]
    kr, ki, kn = (kr0_ref, kr1_ref), (ki0_ref, ki1_ref), (kn0_ref, kn1_ref)
    z = _conv3(v_ref[...], cwv_ref, cbv_ref, seg)
    gates = (_conv3(x1_ref[...], cw1_ref, cb1_ref, seg), _conv3(x2_ref[...], cw2_ref, cb2_ref, seg))
    row0 = lax.broadcasted_iota(jnp.int32, (seq, 1), 0) == 0
    for o in range(HY_ORDER):
        zf = _dot3(fh_ref[...], fl_ref[...], z)
        a, b = zf[:seq], zf[seq:]
        pre = a * kr[o][...] - b * ki[o][...]
        pim = a * ki[o][...] + b * jnp.where(row0, kn[o][0:1, :], kr[o][...])
        conv = _dot3(gh_ref[...], gl_ref[...], jnp.concatenate([pre, pim], axis=0))
        z = gates[o] * (conv + bias_ref[o:o + 1, :] * z)
    o_ref[...] = z.astype(o_ref.dtype)


def _hyena(proj, conv_w, conv_b, spectra, bias, dft, out_prev, *, n_rows, row0, n_b, seq, seg, tc):
    assert HY_ORDER == 2
    kr, ki, kn = spectra
    (fh, fl), (gh, gl) = dft
    rb0 = row0 // seq
    ncb = D_HY // tc
    assert PROJ_HY % tc == 0
    u_spec = lambda part: pl.BlockSpec((seq, tc), lambda cb, b: (rb0 + b, PROJ_HY // tc + part * ncb + cb))
    w_spec = lambda part: pl.BlockSpec((3, tc), lambda cb, b: (0, part * ncb + cb))
    b_spec = lambda part: pl.BlockSpec((1, tc), lambda cb, b: (0, part * ncb + cb))
    k_spec = lambda o: pl.BlockSpec((seq, tc), lambda cb, b: (0, o * ncb + cb), pipeline_mode=pl.Buffered(1))
    n_spec = lambda o: pl.BlockSpec((8, tc), lambda cb, b: (0, o * ncb + cb))
    const = lambda shape: pl.BlockSpec(shape, lambda cb, b: (0, 0), pipeline_mode=pl.Buffered(1))
    in_specs = [u_spec(0), u_spec(1), u_spec(2), w_spec(0), w_spec(1), w_spec(2), b_spec(0), b_spec(1), b_spec(2),
                k_spec(0), k_spec(1), k_spec(0), k_spec(1), n_spec(0), n_spec(1),
                pl.BlockSpec((HY_ORDER, tc), lambda cb, b: (0, cb)),
                const((2 * seq, seq)), const((2 * seq, seq)), const((seq, 2 * seq)), const((seq, 2 * seq))]
    args = [proj, proj, proj, conv_w, conv_w, conv_w, conv_b, conv_b, conv_b, kr, kr, ki, ki, kn, kn, bias,
            fh, fl, gh, gl]
    aliases = {}
    if out_prev is not None:
        in_specs.append(pl.BlockSpec(memory_space=pl.ANY))
        args.append(out_prev)
        aliases = {len(args) - 1: 0}
    return pl.pallas_call(
        functools.partial(_hyena_body, seq=seq, seg=seg, aliased=out_prev is not None),
        grid=(ncb, n_b),
        in_specs=in_specs,
        out_specs=pl.BlockSpec((seq, tc), lambda cb, b: (rb0 + b, cb)),
        out_shape=jax.ShapeDtypeStruct((n_rows, D_HY), BF16),
        input_output_aliases=aliases,
        compiler_params=_cparams(("arbitrary", "arbitrary")),
        name="hyena",
    )(*args)


def _s5_mixer(proj, p, ops, l, dims):
    n_cb, n_cs, n_lb, n_ls = dims
    t_ctx = n_cb * n_cs
    nq_c, nq_l = n_cs // S5_Q, n_ls // S5_Q
    u = lax.optimization_barrier(proj[:, :D_S5]).astype(BF16)

    def to_rows(a, nb, nq):
        a = jnp.transpose(a.reshape(nb, nq, S5_Q, S5_GROUPS, S5_H), (3, 1, 0, 2, 4))
        return a.reshape(S5_GROUPS, nq * nb, S5_Q * S5_H)

    def from_rows(a, nb, nq):
        a = jnp.transpose(a.reshape(S5_GROUPS, nq, nb, S5_Q, S5_H), (2, 1, 3, 0, 4))
        return a.reshape(nb * nq * S5_Q, D_S5)

    u_rows = jnp.concatenate([to_rows(u[:t_ctx], n_cb, nq_c), to_rows(u[t_ctx:], n_lb, nq_l)], axis=1)
    sre, sim = p['state_s5_re'][:, l], p['state_s5_im'][:, l]
    x0 = jnp.transpose(jnp.concatenate([sre[:, 0], sre[:, 1], sim[:, 0], sim[:, 1]], axis=-1), (1, 0, 2))
    y_rows, fin = _s5_scan(u_rows, ops, x0, n_cb, nq_c, n_lb, nq_l)
    y = jnp.concatenate([from_rows(y_rows[:, :nq_c * n_cb], n_cb, nq_c),
                         from_rows(y_rows[:, nq_c * n_cb:], n_lb, nq_l)], axis=0)
    y = _s5_glu(y, p['s5_w_glu'], l)
    fin = jnp.transpose(fin.reshape(S5_GROUPS, n_cb, 4, S5_P), (1, 2, 0, 3))
    return y, fin[:, 0:2], fin[:, 2:4]


def _ssd_mixer(proj, dt_raw, p, l, dims, fin_prev):
    n_cb, n_cs, n_lb, n_ls = dims
    t_ctx = n_cb * n_cs
    t_all = proj.shape[0]
    pad_heads = jnp.zeros((SSD_NGROUPS, LANES - 2 * SSD_HPG), F32)

    def per_group(a):
        return jnp.concatenate([a[0].reshape(SSD_NGROUPS, SSD_HPG), a[1].reshape(SSD_NGROUPS, SSD_HPG),
                                pad_heads], axis=1)[:, None, :]

    dtb_rows = per_group(p['ssd_dt_bias'][l])
    a_rows = per_group(-jnp.exp(p['ssd_a_log'][l]))
    d_rows = jnp.repeat(p['ssd_d'][l], SSD_HEADDIM).reshape(SSD_NGROUPS, 1, SSD_HPG * SSD_HEADDIM)
    dtr = dt_raw.reshape(t_all, 2, SSD_NGROUPS, SSD_HPG)
    dtg = jnp.concatenate([jnp.transpose(dtr, (2, 0, 1, 3)).reshape(SSD_NGROUPS, t_all, 2 * SSD_HPG),
                           jnp.zeros((SSD_NGROUPS, t_all, LANES - 2 * SSD_HPG), F32)], axis=-1)
    cw, cb = p['ssd_conv_w'][l], p['ssd_conv_b'][l][None]
    y, fin = _ssd_scan(proj, dtg, cw, cb, dtb_rows, a_rows, d_rows, None, None, fin_prev,
                       layer=l, row0=0, n_b=n_cb, seq=n_cs, seg=n_cs, want_fin=True)
    y, _ = _ssd_scan(proj, dtg, cw, cb, dtb_rows, a_rows, d_rows, p['state_ssd'][:, l], y, None,
                     layer=l, row0=t_ctx, n_b=n_lb, seq=n_ls, seg=GRID_W, want_fin=False)
    return _ssd_norm(y, proj, p['ssd_norm'][l][None]), fin


def _hyena_mixer(proj, p, l, dims, dfts):
    n_cb, n_cs, n_lb, n_ls = dims
    hcw, hcb = p['hy_conv_w'][l], p['hy_conv_b'][l][None]
    out = None
    cfgs = ((0, n_cb, n_cs, n_cs, 512), (n_cb * n_cs, n_lb, n_ls, GRID_W, 256))
    for (row0, nb, seq, seg, tc), dft in zip(cfgs, dfts):
        hsum, hdiff = _hyena_filters(seq, p['hy_w1'][l], p['hy_b1'][l], p['hy_w2'][l], p['hy_b2'][l], p['hy_w3'][l],
                                     p['hy_freq'][l])
        out = _hyena(proj, hcw, hcb, _hyena_spectra(hsum, hdiff, dft, seq), p['hy_bias'][l], dft, out,
                     n_rows=proj.shape[0], row0=row0, n_b=nb, seq=seq, seg=seg, tc=tc)
    return out


def _in_proj_body(a_ref, wt_ref, o_ref, wbf_ref):
    @pl.when(pl.program_id(1) == 0)
    def _():
        wbf_ref[...] = wt_ref[...].astype(BF16)

    o_ref[...] = lax.dot_general(a_ref[...], wbf_ref[...], (((1,), (1,)), ((), ())), preferred_element_type=F32)


def _in_proj(h, w_in, l, tm=1024):
    m, d = h.shape
    assert N_IN % 8 == 0 and OFF_HY % 8 == 0
    wt = jnp.swapaxes(w_in, 1, 2).reshape(w_in.shape[0] * N_IN, d)
    n_main, n_hy = OFF_DT // PROJ_TN, (N_IN - OFF_HY) // PROJ_TN
    assert OFF_DT % PROJ_TN == 0 and (N_IN - OFF_HY) % PROJ_TN == 0 and OFF_DT + PROJ_TN <= N_IN
    assert (n_main + n_hy + 1) * PROJ_TN == N_PROJ and m % tm == 0

    def w_row(j):
        return jnp.where(j < n_main, j * PROJ_TN, jnp.where(j < n_main + n_hy, OFF_HY + (j - n_main) * PROJ_TN, OFF_DT))

    return pl.pallas_call(
        _in_proj_body,
        grid=(N_PROJ // PROJ_TN, m // tm),
        in_specs=[pl.BlockSpec((tm, d), lambda j, i: (i, 0)),
                  pl.BlockSpec((pl.Element(PROJ_TN), pl.Element(d)),
                               lambda j, i: (pl.multiple_of(l * N_IN + w_row(j), 8), 0))],
        out_specs=pl.BlockSpec((tm, PROJ_TN), lambda j, i: (i, j)),
        out_shape=jax.ShapeDtypeStruct((m, N_PROJ), F32),
        scratch_shapes=[pltpu.VMEM((PROJ_TN, d), BF16)],
        compiler_params=_cparams(("arbitrary", "arbitrary")),
        name="w_in",
    )(h, wt)


def _moe(h_packed, route, wg, wu, wd, lead):
    t_all = h_packed.shape[0]
    token_of_row, row_of_pair, item_e, item_b, item_rows = _moe_dispatch(route, t_all, FFN_MOE_ROWS[-1])
    xs = jnp.take(h_packed, token_of_row, axis=0, mode="clip")
    ys = _ffn(xs, wg, wu, wd, lead, item_e, item_b, item_rows, packed=True, row_variants=FFN_MOE_ROWS)
    return (jnp.take(ys, row_of_pair[:, 0], axis=0, mode="clip"),
            jnp.take(ys, row_of_pair[:, 1], axis=0, mode="clip"))


def kernel(x_prompt, x_sample, state_s5_re, state_s5_im, state_ssd, c, c_ctx, ada_w, ada_b, norm1, norm2, final_norm, w_in, w_out, s5_a_re, s5_a_im, s5_log_dt, s5_b_re, s5_b_im, s5_c_re, s5_c_im, s5_d, s5_w_glu, ssd_conv_w, ssd_conv_b, ssd_dt_bias, ssd_a_log, ssd_d, ssd_norm, hy_conv_w, hy_conv_b, hy_w1, hy_b1, hy_w2, hy_b2, hy_w3, hy_freq, hy_bias, ffn_wg, ffn_wu, ffn_wd, moe_router, moe_wg, moe_wu, moe_wd):
    n_cb, n_cs, d = x_prompt.shape
    n_lb, n_ls, _ = x_sample.shape
    assert d == D_MODEL and n_ls == LAT_ROWS and n_cs % SSD_CHUNK == 0 and n_ls % SSD_CHUNK == 0
    t_ctx, t_lat = n_cb * n_cs, n_lb * n_ls
    t_all = t_ctx + t_lat
    assert t_ctx % LAT_ROWS == 0 and t_all % FFN_ROWS == 0 and 1 + n_lb <= 3
    x = jnp.concatenate([x_prompt.reshape(t_ctx, d), x_sample.reshape(t_lat, d)], axis=0)

    cond = jnp.zeros((8, d), F32).at[0].set(c_ctx).at[1:1 + n_lb].set(c)
    mod = jnp.concatenate(
        [_matmul(cond, ada_w, l, 6 * d, tm=8, tn=512, out_dtype=F32, a_silu=True,
                 bias=ada_b.reshape(DEPTH, 1, 6 * d), name="ada_mod")[None] for l in range(DEPTH)], axis=0)
    mod3 = mod[:, :3].reshape(DEPTH, 3, 1, 6 * d)

    p = dict(state_s5_re=state_s5_re, state_s5_im=state_s5_im, state_ssd=state_ssd,
             s5_a_re=s5_a_re, s5_a_im=s5_a_im, s5_log_dt=s5_log_dt, s5_b_re=s5_b_re, s5_b_im=s5_b_im,
             s5_c_re=s5_c_re, s5_c_im=s5_c_im, s5_d=s5_d, s5_w_glu=s5_w_glu,
             ssd_conv_w=ssd_conv_w, ssd_conv_b=ssd_conv_b, ssd_dt_bias=ssd_dt_bias, ssd_a_log=ssd_a_log,
             ssd_d=ssd_d, ssd_norm=ssd_norm, hy_conv_w=hy_conv_w, hy_conv_b=hy_conv_b, hy_w1=hy_w1, hy_b1=hy_b1,
             hy_w2=hy_w2, hy_b2=hy_b2, hy_w3=hy_w3, hy_freq=hy_freq, hy_bias=hy_bias)
    dims = (n_cb, n_cs, n_lb, n_ls)
    dfts = (_dft_ops(n_cs), _dft_ops(n_ls))
    s5_ops = jax.vmap(_s5_prep)(s5_a_re, s5_a_im, s5_log_dt, s5_b_re, s5_b_im, s5_c_re, s5_c_im, s5_d)
    fin_re, fin_im, fin_ssd = [], [], None
    for l in range(DEPTH):
        m3 = mod3[l]
        h = _norm_mod(x, norm1[l][None], m3, 1, 0, t_ctx)
        proj = _in_proj(h, w_in, l)
        dt_raw = proj[:, PROJ_DT:PROJ_DT + 2 * SSD_HEADS]
        y_s5, f_re, f_im = _s5_mixer(proj, p, tuple(o[l] for o in s5_ops), l, dims)
        y_ssd, fin_ssd = _ssd_mixer(proj, dt_raw, p, l, dims, fin_ssd)
        y_hy = _hyena_mixer(proj, p, l, dims, dfts)
        fin_re.append(f_re)
        fin_im.append(f_im)
        x = _matmul([y_s5, y_ssd, y_hy], w_out, l, d, tm=1024, tn=512, out_dtype=F32, resid=x,
                    gate=m3[:, :, 2 * d:3 * d], n_ctx_rows=t_ctx, name="w_out")
        if l % 2 == 0:
            h = _norm_mod(x, norm2[l][None], m3, 4, 3, t_ctx)
            n_items = t_all // FFN_ROWS
            fs = [_ffn(h, ffn_wg, ffn_wu, ffn_wd, l // 2, jnp.zeros((n_items,), jnp.int32),
                       jnp.arange(n_items, dtype=jnp.int32), jnp.full((n_items,), FFN_ROWS, jnp.int32),
                       packed=False, row_variants=FFN_DENSE_ROWS)]
            route = None
        else:
            router = jnp.concatenate([moe_router[l // 2], jnp.zeros((d, LANES - N_EXPERTS), F32)], axis=1)
            h_packed, route = _norm_mod(x, norm2[l][None], m3, 4, 3, t_ctx, router=router)
            fs = _moe(h_packed, route, moe_wg, moe_wu, moe_wd, l // 2)
        if l < DEPTH - 1:
            x = _resid(x, fs, m3, 5, t_ctx, route=route)
        else:
            y_ctx, y_lat = (_resid(x, fs, m3, 5, t_ctx, route=route, final_w=final_norm[None], rows=rows)
                            for rows in ((0, t_ctx), (t_ctx, t_lat)))

    y_prompt = y_ctx.reshape(n_cb, n_cs, d)
    y_sample = y_lat.reshape(n_lb, n_ls, d)
    return (y_prompt, y_sample, jnp.stack(fin_re, axis=1), jnp.stack(fin_im, axis=1), fin_ssd)
```

```python
import functools
import math

import jax
import jax.numpy as jnp
from jax import lax
from jax.experimental import pallas as pl
from jax.experimental.pallas import tpu as pltpu

F32 = jnp.float32
BF16 = jnp.bfloat16
HIGHEST = lax.Precision.HIGHEST

D_MODEL = 4096
DEPTH = 2
GRID_W = 64
EPS = 1e-6
D_S5 = D_MODEL // 4
D_SSD = D_MODEL // 2
D_HY = D_MODEL - D_S5 - D_SSD
S5_H = 16
S5_GROUPS = D_S5 // S5_H
S5_P = 64
S5_Q = 16
SSD_HEADDIM = 64
SSD_HEADS = D_SSD // SSD_HEADDIM
SSD_NGROUPS = 4
SSD_HPG = SSD_HEADS // SSD_NGROUPS
SSD_DSTATE = 128
SSD_CHUNK = 128
SSD_CONV_DIM = D_SSD + 2 * SSD_NGROUPS * SSD_DSTATE
HY_ORDER = 2
HY_BANDS = 16
HY_MIN_DECAY = math.log(1e-2) / 1.5
HY_MAX_DECAY = math.log(1e-2) / 0.3
D_FF = 14336
N_EXPERTS = 8
OFF_Z = D_S5
OFF_XBC = OFF_Z + D_SSD
OFF_DT = OFF_XBC + SSD_CONV_DIM
OFF_HY = OFF_DT + 2 * SSD_HEADS
N_IN = OFF_HY + (HY_ORDER + 1) * D_HY

LANES = 128
VMEM_LIMIT = 56 * 1024 * 1024
LAT_ROWS = 1024
PROJ_TN = 512
N_PROJ = -(-N_IN // PROJ_TN) * PROJ_TN
PROJ_HY = OFF_DT
PROJ_DT = PROJ_HY + (HY_ORDER + 1) * D_HY


def _cparams(sem, vmem=VMEM_LIMIT):
    return pltpu.CompilerParams(dimension_semantics=sem, vmem_limit_bytes=vmem)


def _mod_row(i, tm, n_ctx_rows):
    return jnp.maximum((i * tm - n_ctx_rows) // LAT_ROWS + 1, 0)


def _mm_body(*refs, n_a, a_silu, has_bias, has_resid):
    a_refs, w_ref = refs[:n_a], refs[n_a]
    k = n_a + 1
    bias_ref = resid_ref = gate_ref = None
    if has_bias:
        bias_ref = refs[k]
        k += 1
    if has_resid:
        resid_ref, gate_ref = refs[k], refs[k + 1]
        k += 2
    o_ref, wbf_ref = refs[k], refs[k + 1]

    @pl.when(pl.program_id(1) == 0)
    def _():
        wbf_ref[...] = w_ref[...].astype(BF16)

    acc = None
    k0 = 0
    for a_ref in a_refs:
        a = a_ref[...]
        if a_silu:
            a = jax.nn.silu(a).astype(BF16)
        part = jnp.dot(a, wbf_ref[k0:k0 + a.shape[1], :], preferred_element_type=F32)
        acc = part if acc is None else acc + part
        k0 += a.shape[1]
    if has_bias:
        acc = acc + bias_ref[...]
    if has_resid:
        acc = resid_ref[...] + gate_ref[...] * acc
    o_ref[...] = acc.astype(o_ref.dtype)


def _matmul(a, w, w_lead, n_cols, *, tm, tn, out_dtype, a_silu=False, bias=None, resid=None, gate=None,
            n_ctx_rows=0, name="matmul"):
    a_list = list(a) if isinstance(a, (list, tuple)) else [a]
    m = a_list[0].shape[0]
    kdim = sum(x.shape[1] for x in a_list)
    assert m % tm == 0 and n_cols % tn == 0 and kdim == w.shape[-2]
    grid = (n_cols // tn, m // tm)
    w_spec = pl.BlockSpec((None, kdim, tn), lambda j, i: (w_lead, 0, j))
    in_specs = [pl.BlockSpec((tm, x.shape[1]), lambda j, i: (i, 0)) for x in a_list] + [w_spec]
    args = a_list + [w]
    if bias is not None:
        in_specs.append(pl.BlockSpec((None, 1, tn), lambda j, i: (w_lead, 0, j)))
        args.append(bias)
    if resid is not None:
        in_specs.append(pl.BlockSpec((tm, tn), lambda j, i: (i, j)))
        in_specs.append(pl.BlockSpec((None, 1, tn), lambda j, i: (_mod_row(i, tm, n_ctx_rows), 0, j)))
        args += [resid, gate]
    return pl.pallas_call(
        functools.partial(_mm_body, n_a=len(a_list), a_silu=a_silu, has_bias=bias is not None,
                          has_resid=resid is not None),
        grid=grid,
        in_specs=in_specs,
        out_specs=pl.BlockSpec((tm, tn), lambda j, i: (i, j)),
        out_shape=jax.ShapeDtypeStruct((m, n_cols), out_dtype),
        scratch_shapes=[pltpu.VMEM((kdim, tn), BF16)],
        compiler_params=_cparams(("arbitrary", "arbitrary")),
        name=name,
    )(*args)


def _norm_mod_body(x_ref, w_ref, sc_ref, sh_ref, *rest, with_router):
    x = x_ref[...]
    ms = jnp.mean(x * x, axis=-1, keepdims=True)
    y = x * lax.rsqrt(ms + EPS) * w_ref[...]
    h = y * (1.0 + sc_ref[...]) + sh_ref[...]
    if not with_router:
        (o_ref,) = rest
        o_ref[...] = h.astype(o_ref.dtype)
        return
    r_ref, o_ref, route_ref = rest
    half = h.shape[1] // 2
    hb = h.astype(BF16)
    lo = lax.bitcast_convert_type(hb[:, :half].astype(F32), jnp.uint32)
    hi = lax.bitcast_convert_type(hb[:, half:].astype(F32), jnp.uint32)
    o_ref[...] = lax.bitcast_convert_type((lo >> 16) | hi, F32)
    logits = jnp.dot(h, r_ref[...], preferred_element_type=F32, precision=HIGHEST)
    lane = lax.broadcasted_iota(jnp.int32, logits.shape, 1).astype(F32)
    neg = jnp.float32(-jnp.inf)
    lg = jnp.where(lane < N_EXPERTS, logits, neg)
    m1 = jnp.max(lg, axis=-1, keepdims=True)
    i1 = jnp.min(jnp.where(lg == m1, lane, float(LANES)), axis=-1, keepdims=True)
    lg2 = jnp.where(lane == i1, neg, lg)
    m2 = jnp.max(lg2, axis=-1, keepdims=True)
    i2 = jnp.min(jnp.where(lg2 == m2, lane, float(LANES)), axis=-1, keepdims=True)
    e = jnp.exp(m2 - m1)
    g1 = 1.0 / (1.0 + e)
    g2 = e / (1.0 + e)
    route = jnp.where(lane == 0, i1, jnp.where(lane == 1, i2, jnp.where(lane == 2, g1, jnp.where(lane == 3, g2, 0.0))))
    route_ref[...] = route


def _norm_mod(x, w_row, mod3, k_scale, k_shift, n_ctx_rows, router=None, tm=256):
    m, d = x.shape
    in_specs = [
        pl.BlockSpec((tm, d), lambda i: (i, 0)),
        pl.BlockSpec((1, d), lambda i: (0, 0)),
        pl.BlockSpec((None, 1, d), lambda i: (_mod_row(i, tm, n_ctx_rows), 0, k_scale)),
        pl.BlockSpec((None, 1, d), lambda i: (_mod_row(i, tm, n_ctx_rows), 0, k_shift)),
    ]
    args = [x, w_row, mod3, mod3]
    out_specs = pl.BlockSpec((tm, d), lambda i: (i, 0))
    out_shape = jax.ShapeDtypeStruct((m, d), BF16)
    if router is not None:
        in_specs.append(pl.BlockSpec((d, LANES), lambda i: (0, 0)))
        args.append(router)
        out_specs = [pl.BlockSpec((tm, d // 2), lambda i: (i, 0)), pl.BlockSpec((tm, LANES), lambda i: (i, 0))]
        out_shape = [jax.ShapeDtypeStruct((m, d // 2), F32), jax.ShapeDtypeStruct((m, LANES), F32)]
    return pl.pallas_call(
        functools.partial(_norm_mod_body, with_router=router is not None),
        grid=(m // tm,),
        in_specs=in_specs,
        out_specs=out_specs,
        out_shape=out_shape,
        compiler_params=_cparams(("arbitrary",)),
        name="norm_mod",
    )(*args)


def _resid_body(*refs, n_f, final):
    x_ref, f_refs, g_ref = refs[0], refs[1:1 + n_f], refs[1 + n_f]
    rest = refs[2 + n_f:]
    if n_f == 1:
        f = f_refs[0][...]
    else:
        route_ref, rest = rest[0], rest[1:]
        f = route_ref[:, 2:3] * f_refs[0][...] + route_ref[:, 3:4] * f_refs[1][...]
    y = x_ref[...] + g_ref[...] * f
    if final:
        w_ref, o_ref = rest
        ms = jnp.mean(y * y, axis=-1, keepdims=True)
        y = y * lax.rsqrt(ms + EPS) * w_ref[...]
    else:
        (o_ref,) = rest
    o_ref[...] = y


def _resid(x, fs, mod3, k_gate, n_ctx_rows, route=None, final_w=None, rows=None, tm=256):
    d = x.shape[1]
    row0, m = rows if rows is not None else (0, x.shape[0])
    assert row0 % tm == 0 and m % tm == 0
    t0 = row0 // tm
    tok = pl.BlockSpec((tm, d), lambda i: (t0 + i, 0))
    in_specs = [tok] + [tok] * len(fs) + [
        pl.BlockSpec((None, 1, d), lambda i: (_mod_row(t0 + i, tm, n_ctx_rows), 0, k_gate))]
    args = [x, *fs, mod3]
    if route is not None:
        in_specs.append(pl.BlockSpec((tm, LANES), lambda i: (t0 + i, 0)))
        args.append(route)
    if final_w is not None:
        in_specs.append(pl.BlockSpec((1, d), lambda i: (0, 0)))
        args.append(final_w)
    return pl.pallas_call(
        functools.partial(_resid_body, n_f=len(fs), final=final_w is not None),
        grid=(m // tm,),
        in_specs=in_specs,
        out_specs=pl.BlockSpec((tm, d), lambda i: (i, 0)),
        out_shape=jax.ShapeDtypeStruct((m, d), F32),
        compiler_params=_cparams(("arbitrary",)),
        name="resid",
    )(*args)


FFN_ROWS = 1024
FFN_TF = 256
FFN_KC = 1024
FFN_NC = 512
FFN_VMEM_LIMIT = 62 * 1024 * 1024
FFN_DENSE_ROWS = (FFN_ROWS,)
FFN_MOE_ROWS = tuple(range(128, FFN_ROWS + 1, 128))


def _ffn_body(ie_ref, ib_ref, nv_ref, x_ref, wg_ref, wu_ref, wd_ref, o_ref, *, packed, row_variants):
    it = pl.program_id(0)
    f = pl.program_id(1)
    nv = nv_ref[it]
    d = wg_ref.shape[0]
    gran = row_variants[0]

    @pl.when(f == 0)
    def _():
        o_ref[...] = jnp.zeros_like(o_ref)

    def x_chunk(m, kc):
        if not packed:
            return x_ref[0:m, kc * FFN_KC:(kc + 1) * FFN_KC]
        per_half = (d // 2) // FFN_KC
        c0 = (kc % per_half) * FFN_KC
        w = lax.bitcast_convert_type(x_ref[0:m, c0:c0 + FFN_KC], jnp.uint32)
        bits = (w << 16) if kc < per_half else (w & jnp.uint32(0xFFFF0000))
        return lax.bitcast_convert_type(bits, F32).astype(BF16)

    def compute(m):
        g = u = None
        for kc in range(d // FFN_KC):
            ks = slice(kc * FFN_KC, (kc + 1) * FFN_KC)
            xk = x_chunk(m, kc)
            pg = jnp.dot(xk, wg_ref[ks, :].astype(BF16), preferred_element_type=F32)
            pu = jnp.dot(xk, wu_ref[ks, :].astype(BF16), preferred_element_type=F32)
            g = pg if g is None else g + pg
            u = pu if u is None else u + pu
        a = (jax.nn.silu(g) * u).astype(BF16)
        for nc in range(d // FFN_NC):
            ns = slice(nc * FFN_NC, (nc + 1) * FFN_NC)
            o_ref[0:m, ns] += jnp.dot(a, wd_ref[:, ns].astype(BF16), preferred_element_type=F32)

    n_gran = (nv + gran - 1) // gran
    for m in row_variants:
        @pl.when(n_gran == -(-m // gran))
        def _(m=m):
            compute(m)


def _ffn(xs, wg, wu, wd, lead, item_expert, item_block, item_rows, *, packed, row_variants):
    r = xs.shape[0]
    d = wg.shape[-2]
    n_items = item_expert.shape[0]
    dff = wg.shape[-1]
    n_f = dff // FFN_TF
    block_rows = row_variants[-1]
    assert all(m % row_variants[0] == 0 for m in row_variants[:-1]) and r % block_rows == 0

    def f_idx(it, f, nv):
        return jnp.where(nv[it] > 0, f, n_f - 1)

    if wg.ndim == 4:
        wgu_spec = pl.BlockSpec((None, None, d, FFN_TF), lambda it, f, ie, ib, nv: (lead, ie[it], 0, f_idx(it, f, nv)))
        wd_spec = pl.BlockSpec((None, None, FFN_TF, d), lambda it, f, ie, ib, nv: (lead, ie[it], f_idx(it, f, nv), 0))
    else:
        wgu_spec = pl.BlockSpec((None, d, FFN_TF), lambda it, f, ie, ib, nv: (lead, 0, f_idx(it, f, nv)))
        wd_spec = pl.BlockSpec((None, FFN_TF, d), lambda it, f, ie, ib, nv: (lead, f_idx(it, f, nv), 0))
    grid_spec = pltpu.PrefetchScalarGridSpec(
        num_scalar_prefetch=3,
        grid=(n_items, n_f),
        in_specs=[
            pl.BlockSpec((block_rows, xs.shape[1]), lambda it, f, ie, ib, nv: (ib[it], 0),
                         pipeline_mode=pl.Buffered(1)),
            wgu_spec, wgu_spec, wd_spec,
        ],
        out_specs=pl.BlockSpec((block_rows, d), lambda it, f, ie, ib, nv: (ib[it], 0), pipeline_mode=pl.Buffered(1)),
    )
    return pl.pallas_call(
        functools.partial(_ffn_body, packed=packed, row_variants=row_variants),
        grid_spec=grid_spec,
        out_shape=jax.ShapeDtypeStruct((r, d), F32),
        compiler_params=_cparams(("arbitrary", "arbitrary"), vmem=FFN_VMEM_LIMIT),
        name="ffn",
    )(item_expert, item_block, item_rows, xs, wg, wu, wd)


def _moe_dispatch(route, n_tokens, block):
    e_idx = route[:, :2].astype(jnp.int32)
    flat_e = e_idx.reshape(-1)
    n_pairs = flat_e.shape[0]
    onehot = (flat_e[:, None] == jnp.arange(N_EXPERTS)[None, :]).astype(jnp.int32)
    seen = jnp.cumsum(onehot, axis=0)
    counts = seen[-1]
    rank = jnp.sum(onehot * seen, axis=1) - 1
    blocks_per = (counts + block - 1) // block
    blk_end = jnp.cumsum(blocks_per)
    blk_start = blk_end - blocks_per
    dest = jnp.sum(onehot * blk_start[None, :], axis=1) * block + rank
    n_items = (n_pairs + N_EXPERTS * (block - 1)) // block
    n_rows = n_items * block
    token_of_row = jnp.zeros((n_rows,), jnp.int32).at[dest].set(jnp.arange(n_pairs, dtype=jnp.int32) // 2)
    row_of_pair = dest.reshape(n_tokens, 2)
    blk = jnp.arange(n_items, dtype=jnp.int32)
    n_real = blk_end[-1]
    blk_c = jnp.minimum(blk, n_real - 1)
    item_e = jnp.minimum(jnp.sum(blk_c[:, None] >= blk_end[None, :], axis=1), N_EXPERTS - 1).astype(jnp.int32)
    rows_left = counts[item_e] - (blk_c - blk_start[item_e]) * block
    item_rows = jnp.where(blk < n_real, jnp.clip(rows_left, 0, block), 0).astype(jnp.int32)
    return token_of_row, row_of_pair, item_e, blk, item_rows


def _s5_prep(a_re, a_im, log_dt, b_re, b_im, c_re, c_im, d):
    q, g, p, h = S5_Q, S5_GROUPS, S5_P, S5_H
    dt = jnp.exp(log_dt)[..., None]
    k = jnp.arange(q + 1, dtype=F32)
    mag = jnp.exp((a_re * dt)[..., None] * k)
    ang = (a_im * dt)[..., None] * k
    pw_re, pw_im = mag * jnp.cos(ang), mag * jnp.sin(ang)
    ab_re, ab_im = pw_re[..., 1], pw_im[..., 1]
    den = a_re * a_re + a_im * a_im
    q_re = ((ab_re - 1.0) * a_re + ab_im * a_im) / den
    q_im = (ab_im * a_re - (ab_re - 1.0) * a_im) / den
    bb_re = q_re[..., None] * b_re[None] - q_im[..., None] * b_im[None]
    bb_im = q_re[..., None] * b_im[None] + q_im[..., None] * b_re[None]
    c_re_t = jnp.swapaxes(c_re, 1, 2)[None, ..., None]
    c_im_t = jnp.swapaxes(c_im, 1, 2)[None, ..., None]
    cp_re = c_re_t * pw_re[:, :, :, None, :] - c_im_t * pw_im[:, :, :, None, :]
    cp_im = c_re_t * pw_im[:, :, :, None, :] + c_im_t * pw_re[:, :, :, None, :]
    cpx_re = jnp.swapaxes(cp_re, 3, 4)[..., None]
    cpx_im = jnp.swapaxes(cp_im, 3, 4)[..., None]
    bbx_re = bb_re[:, :, :, None, None, :]
    bbx_im = bb_im[:, :, :, None, None, :]
    kern = jnp.transpose(jnp.sum(cpx_re * bbx_re - cpx_im * bbx_im, axis=2), (0, 2, 1, 3, 4))
    ii = jnp.arange(q)[:, None]
    jj = jnp.arange(q)[None, :]
    lag = ii - jj
    kf = jnp.where((lag >= 0)[..., None, None, None], kern[0][jnp.clip(lag, 0, q)], 0.0)
    kb = jnp.where((lag <= 0)[..., None, None, None], kern[1][jnp.clip(-lag, 0, q)], 0.0)
    m_intra = jnp.transpose(kf + kb, (2, 1, 4, 0, 3)).reshape(g, q * h, q * h)
    pf_re, pf_im = pw_re[0][..., q - 1 - jnp.arange(q)], pw_im[0][..., q - 1 - jnp.arange(q)]
    pb_re, pb_im = pw_re[1][..., :q], pw_im[1][..., :q]

    def contrib(p_re, p_im, b_r, b_i):
        w_re = p_re[:, :, :, None] * b_r[:, :, None, :] - p_im[:, :, :, None] * b_i[:, :, None, :]
        w_im = p_re[:, :, :, None] * b_i[:, :, None, :] + p_im[:, :, :, None] * b_r[:, :, None, :]
        to_rows = lambda w: jnp.transpose(w, (0, 2, 3, 1)).reshape(g, q * h, p)
        return to_rows(w_re), to_rows(w_im)

    wsf_re, wsf_im = contrib(pf_re, pf_im, bb_re[0], bb_im[0])
    wsb_re, wsb_im = contrib(pb_re, pb_im, bb_re[1], bb_im[1])
    w_state = jnp.concatenate([wsf_re, wsb_re, wsf_im, wsb_im], axis=-1)
    to_cols = lambda w: jnp.transpose(w, (0, 1, 3, 2)).reshape(g, p, q * h)
    fi = jnp.arange(q) + 1
    bi = q - jnp.arange(q)
    w_off = jnp.concatenate([to_cols(cp_re[0][..., fi]), to_cols(cp_re[1][..., bi]),
                             to_cols(-cp_im[0][..., fi]), to_cols(-cp_im[1][..., bi])], axis=1)
    aq_re, aq_im = pw_re[..., q], pw_im[..., q]
    a1 = jnp.concatenate([aq_re[0], aq_re[1]], axis=-1)[:, None, :]
    a2 = jnp.concatenate([aq_im[0], aq_im[1]], axis=-1)[:, None, :]
    d_row = jnp.tile(d.reshape(g, 1, h), (1, 1, q))
    return m_intra.astype(BF16), w_state.astype(BF16), w_off.astype(BF16), a1, a2, d_row


def _s5_body(u_ref, mi_ref, ws_ref, wo_ref, a1_ref, a2_ref, d_ref, x0_ref, y_ref, fin_ref, con_ref, sp_ref,
             *, n_ctx_b, n_ctx_c, n_lat_b, n_lat_c):
    half = 2 * S5_P
    ub = u_ref[...]
    u = ub.astype(F32)
    con_ref[...] = jnp.dot(ub, ws_ref[...], preferred_element_type=F32)

    def scan(row0, nb, nc, s_re, s_im):
        ar = jnp.broadcast_to(a1_ref[...], (nb, half))
        ai = jnp.broadcast_to(a2_ref[...], (nb, half))
        is_fwd = lax.broadcasted_iota(jnp.int32, (nb, half), 1) < S5_P
        for k in range(nc):
            rf = slice(row0 + k * nb, row0 + (k + 1) * nb)
            rb = slice(row0 + (nc - 1 - k) * nb, row0 + (nc - k) * nb)
            sp_ref[rf, 0:S5_P] = s_re[:, :S5_P]
            sp_ref[rb, S5_P:half] = s_re[:, S5_P:]
            sp_ref[rf, half:half + S5_P] = s_im[:, :S5_P]
            sp_ref[rb, half + S5_P:2 * half] = s_im[:, S5_P:]
            c_re = jnp.where(is_fwd, con_ref[rf, 0:half], con_ref[rb, 0:half])
            c_im = jnp.where(is_fwd, con_ref[rf, half:2 * half], con_ref[rb, half:2 * half])
            s_re, s_im = ar * s_re - ai * s_im + c_re, ar * s_im + ai * s_re + c_im
        return s_re, s_im

    zeros = jnp.zeros((n_ctx_b, half), F32)
    fin_ref[:, 0:half], fin_ref[:, half:2 * half] = scan(0, n_ctx_b, n_ctx_c, zeros, zeros)
    scan(n_ctx_b * n_ctx_c, n_lat_b, n_lat_c, x0_ref[:, 0:half], x0_ref[:, half:2 * half])
    y = jnp.dot(ub, mi_ref[...], preferred_element_type=F32)
    y = y + jnp.dot(sp_ref[...].astype(BF16), wo_ref[...], preferred_element_type=F32)
    y_ref[...] = (y + u * d_ref[...]).astype(y_ref.dtype)


def _s5_scan(u_rows, ops, x0, n_ctx_b, n_ctx_c, n_lat_b, n_lat_c):
    g, r, w = u_rows.shape
    m_intra, w_state, w_off, a1, a2, d_row = ops
    mat = pl.BlockSpec((None, w, w), lambda i: (i, 0, 0))
    row = pl.BlockSpec((None, 1, w), lambda i: (i, 0, 0))
    return pl.pallas_call(
        functools.partial(_s5_body, n_ctx_b=n_ctx_b, n_ctx_c=n_ctx_c, n_lat_b=n_lat_b, n_lat_c=n_lat_c),
        grid=(g,),
        in_specs=[pl.BlockSpec((None, r, w), lambda i: (i, 0, 0)), mat, mat, mat,
                  pl.BlockSpec((None, 1, w // 2), lambda i: (i, 0, 0)),
                  pl.BlockSpec((None, 1, w // 2), lambda i: (i, 0, 0)), row,
                  pl.BlockSpec((None, n_lat_b, w), lambda i: (i, 0, 0))],
        out_specs=[pl.BlockSpec((None, r, w), lambda i: (i, 0, 0)),
                   pl.BlockSpec((None, n_ctx_b, w), lambda i: (i, 0, 0))],
        out_shape=[jax.ShapeDtypeStruct((g, r, w), BF16), jax.ShapeDtypeStruct((g, n_ctx_b, w), F32)],
        scratch_shapes=[pltpu.VMEM((r, w), F32), pltpu.VMEM((r, w), F32)],
        compiler_params=_cparams(("arbitrary",)),
        name="s5_scan",
    )(u_rows, m_intra, w_state, w_off, a1, a2, d_row, x0)


def _s5_glu_body(y_ref, w_ref, o_ref, wbf_ref):
    @pl.when(pl.program_id(0) == 0)
    def _():
        wbf_ref[...] = w_ref[...].astype(BF16)

    y = jax.nn.gelu(y_ref[...].astype(F32))
    z = jnp.dot(y.astype(BF16), wbf_ref[...], preferred_element_type=F32)
    o_ref[...] = (y * jax.nn.sigmoid(z)).astype(o_ref.dtype)


def _s5_glu(y, w_glu, lead, tm=512):
    m, d = y.shape
    return pl.pallas_call(
        _s5_glu_body,
        grid=(m // tm,),
        in_specs=[pl.BlockSpec((tm, d), lambda i: (i, 0)), pl.BlockSpec((None, d, d), lambda i: (lead, 0, 0))],
        out_specs=pl.BlockSpec((tm, d), lambda i: (i, 0)),
        out_shape=jax.ShapeDtypeStruct((m, d), BF16),
        scratch_shapes=[pltpu.VMEM((d, d), BF16)],
        compiler_params=_cparams(("arbitrary",)),
        name="s5_glu",
    )(y, w_glu)


def _conv3(u, w_ref, b_ref, seg):
    n = u.shape[0]
    assert seg & (seg - 1) == 0
    pos = lax.broadcasted_iota(jnp.int32, u.shape, 0) & (seg - 1)
    prev = jnp.where(pos == 0, 0.0, pltpu.roll(u, 1, 0))
    nxt = jnp.where(pos == seg - 1, 0.0, pltpu.roll(u, n - 1, 0))
    return b_ref[...] + w_ref[0:1, :] * prev + w_ref[1:2, :] * u + w_ref[2:3, :] * nxt


def _softplus(x):
    return jnp.maximum(x, 0.0) + jnp.log(1.0 + jnp.exp(-jnp.abs(x)))


def _ssd_body(*refs, seq, seg, has_init, want_fin, n_aliased):
    (x_ref, b_ref, c_ref, dt_ref, cwx_ref, cwb_ref, cwc_ref, cbx_ref, cbb_ref, cbc_ref,
     dtb_ref, arow_ref, drow_ref) = refs[:13]
    k = 13
    init_ref = None
    if has_init:
        init_ref = refs[k]
        k += 1
    k += n_aliased
    y_ref = refs[k]
    k += 1
    fin_ref = None
    if want_fin:
        fin_ref = refs[k]
        k += 1
    xc_ref, bc_ref, cc_ref, stf_ref, stb_ref, xsb_ref, rcsx_ref = refs[k:k + 7]
    q, p, hpg = SSD_CHUNK, SSD_HEADDIM, SSD_HPG
    wid = hpg * p
    nc = seq // q

    xc_ref[...] = jax.nn.silu(_conv3(x_ref[...], cwx_ref, cbx_ref, seg))
    bc_ref[...] = jax.nn.silu(_conv3(b_ref[...], cwb_ref, cbb_ref, seg))
    cc_ref[...] = jax.nn.silu(_conv3(c_ref[...], cwc_ref, cbc_ref, seg))
    if has_init:
        stf_ref[...] = jnp.concatenate([init_ref[0, r] for r in range(hpg)], axis=0).T
        stb_ref[...] = jnp.concatenate([init_ref[1, r] for r in range(hpg)], axis=0).T
    else:
        stf_ref[...] = jnp.zeros_like(stf_ref)
        stb_ref[...] = jnp.zeros_like(stb_ref)

    ii = lax.broadcasted_iota(jnp.int32, (q, q), 0)
    jj = lax.broadcasted_iota(jnp.int32, (q, q), 1)
    tri_lo = (ii >= jj).astype(BF16)
    er = lax.broadcasted_iota(jnp.int32, (LANES, 2 * wid), 0)
    ec = lax.broadcasted_iota(jnp.int32, (LANES, 2 * wid), 1) // p
    expand = (er == ec).astype(BF16)
    neg = jnp.float32(-jnp.inf)
    bdot = functools.partial(jnp.dot, preferred_element_type=F32)

    def split3(v):
        v1 = v.astype(BF16)
        r1 = v - v1.astype(F32)
        v2 = r1.astype(BF16)
        return v1, v2, (r1 - v2.astype(F32)).astype(BF16)

    def cumsum_rows(v):
        r = bdot(tri_lo, jnp.concatenate(split3(v), axis=1))
        n = v.shape[1]
        return r[:, :n] + r[:, n:2 * n] + r[:, 2 * n:]

    def expand_heads(pieces, cols):
        r = bdot(jnp.concatenate(pieces, axis=0), expand[:, cols])
        acc = r[0:q]
        for i in range(1, len(pieces)):
            acc = acc + r[i * q:(i + 1) * q]
        return acc

    def fwd(c, carry):
        r0 = pl.multiple_of(c * q, q)
        dt = _softplus(dt_ref[pl.ds(r0, q), :] + dtb_ref[...])
        adt = dt * arow_ref[...]
        cs = cumsum_rows(adt)
        rcs = cs[q - 1:q, :] - cs + adt
        lane = lax.broadcasted_iota(jnp.int32, (q, LANES), 1)
        cs_t = jnp.where(lane < hpg, cs, rcs).T
        x = xc_ref[pl.ds(r0, q), :]
        bm = bc_ref[pl.ds(r0, q), :]
        cm = cc_ref[pl.ds(r0, q), :].astype(BF16)
        cb = lax.dot_general(cm, bm.astype(BF16), (((1,), (1,)), ((), ())), preferred_element_type=F32)
        dtx = expand_heads(split3(dt)[:2], slice(0, 2 * wid))
        csx = expand_heads(split3(cs), slice(0, wid))
        rcsx = expand_heads(split3(rcs), slice(wid, 2 * wid))
        xs_f = x * dtx[:, :wid]
        xs_b = x * dtx[:, wid:]
        xsb_ref[pl.ds(r0, q), :] = xs_b
        rcsx_ref[pl.ds(r0, q), :] = rcsx
        parts = []
        for r in range(hpg):
            cols = slice(r * p, (r + 1) * p)
            lf = jnp.exp(jnp.where(ii >= jj, cs[:, r:r + 1] - cs_t[r:r + 1, :], neg))
            lb = jnp.exp(jnp.where(jj >= ii, rcs[:, hpg + r:hpg + r + 1] - cs_t[hpg + r:hpg + r + 1, :], neg))
            lhs = jnp.concatenate([cb * lf, cb * lb], axis=1).astype(BF16)
            rhs = jnp.concatenate([xs_f[:, cols], xs_b[:, cols]], axis=0).astype(BF16)
            parts.append(bdot(lhs, rhs))
        y = jnp.concatenate(parts, axis=1)
        st = stf_ref[...]
        y = y + bdot(cm, st.astype(BF16)) * jnp.exp(csx) + drow_ref[...] * x
        y_ref[pl.ds(r0, q), :] = y
        tot = csx[q - 1:q, :]
        dec = jnp.exp(tot - csx)
        stf_ref[...] = st * jnp.exp(tot) + bdot(bm.T.astype(BF16), (xs_f * dec).astype(BF16))
        return carry

    def bwd(k_it, carry):
        c = nc - 1 - k_it
        r0 = pl.multiple_of(c * q, q)
        rcsx = rcsx_ref[pl.ds(r0, q), :]
        xs_b = xsb_ref[pl.ds(r0, q), :]
        bm = bc_ref[pl.ds(r0, q), :]
        cm = cc_ref[pl.ds(r0, q), :].astype(BF16)
        st = stb_ref[...]
        y_ref[pl.ds(r0, q), :] += bdot(cm, st.astype(BF16)) * jnp.exp(rcsx)
        tot = rcsx[0:1, :]
        dec = jnp.exp(tot - rcsx)
        stb_ref[...] = st * jnp.exp(tot) + bdot(bm.T.astype(BF16), (xs_b * dec).astype(BF16))
        return carry

    lax.fori_loop(0, nc, fwd, 0)
    lax.fori_loop(0, nc, bwd, 0)
    if want_fin:
        fin_f, fin_b = stf_ref[...].T, stb_ref[...].T
        for r in range(hpg):
            fin_ref[0, r] = fin_f[r * p:(r + 1) * p, :]
            fin_ref[1, r] = fin_b[r * p:(r + 1) * p, :]


def _ssd_scan(proj, dtg, conv_w, conv_b, dtb_rows, a_rows, d_rows, init, y_prev, fin_prev, *, layer, row0, n_b,
              seq, seg, want_fin):
    q, p, hpg, ng, ns = SSD_CHUNK, SSD_HEADDIM, SSD_HPG, SSD_NGROUPS, SSD_DSTATE
    wid = hpg * p
    rb0 = row0 // seq
    xcol0 = OFF_XBC // wid
    bcol0 = (OFF_XBC + D_SSD) // ns
    ccol0 = bcol0 + ng
    in_specs = [
        pl.BlockSpec((seq, wid), lambda b, g: (rb0 + b, xcol0 + g)),
        pl.BlockSpec((seq, ns), lambda b, g: (rb0 + b, bcol0 + g)),
        pl.BlockSpec((seq, ns), lambda b, g: (rb0 + b, ccol0 + g)),
        pl.BlockSpec((None, seq, LANES), lambda b, g: (g, rb0 + b, 0)),
        pl.BlockSpec((3, wid), lambda b, g: (0, g)),
        pl.BlockSpec((3, ns), lambda b, g: (0, D_SSD // ns + g)),
        pl.BlockSpec((3, ns), lambda b, g: (0, D_SSD // ns + ng + g)),
        pl.BlockSpec((1, wid), lambda b, g: (0, g)),
        pl.BlockSpec((1, ns), lambda b, g: (0, D_SSD // ns + g)),
        pl.BlockSpec((1, ns), lambda b, g: (0, D_SSD // ns + ng + g)),
        pl.BlockSpec((None, 1, LANES), lambda b, g: (g, 0, 0)),
        pl.BlockSpec((None, 1, LANES), lambda b, g: (g, 0, 0)),
        pl.BlockSpec((None, 1, wid), lambda b, g: (g, 0, 0)),
    ]
    args = [proj, proj, proj, dtg, conv_w, conv_w, conv_w, conv_b, conv_b, conv_b, dtb_rows, a_rows, d_rows]
    if init is not None:
        in_specs.append(pl.BlockSpec((None, 2, hpg, p, ns), lambda b, g: (b, 0, g, 0, 0)))
        args.append(init)
    aliases = {}
    for k_out, prev in enumerate((y_prev, fin_prev)):
        if prev is not None:
            in_specs.append(pl.BlockSpec(memory_space=pl.ANY))
            args.append(prev)
            aliases[len(args) - 1] = k_out
    out_specs = [pl.BlockSpec((seq, wid), lambda b, g: (rb0 + b, g))]
    out_shape = [jax.ShapeDtypeStruct((proj.shape[0], D_SSD), F32)]
    if want_fin:
        out_specs.append(pl.BlockSpec((None, None, 2, hpg, p, ns), lambda b, g: (b, layer, 0, g, 0, 0)))
        out_shape.append(jax.ShapeDtypeStruct((n_b, DEPTH, 2, SSD_HEADS, p, ns), F32))
    else:
        assert fin_prev is None
    res = pl.pallas_call(
        functools.partial(_ssd_body, seq=seq, seg=seg, has_init=init is not None, want_fin=want_fin,
                          n_aliased=len(aliases)),
        grid=(n_b, ng),
        in_specs=in_specs,
        out_specs=out_specs,
        out_shape=out_shape,
        input_output_aliases=aliases,
        scratch_shapes=[pltpu.VMEM((seq, wid), F32), pltpu.VMEM((seq, ns), F32), pltpu.VMEM((seq, ns), F32),
                        pltpu.VMEM((ns, wid), F32), pltpu.VMEM((ns, wid), F32),
                        pltpu.VMEM((seq, wid), F32), pltpu.VMEM((seq, wid), F32)],
        compiler_params=_cparams(("arbitrary", "arbitrary")),
        name="ssd_scan",
    )(*args)
    return res if want_fin else (res[0], None)


def _ssd_norm_body(y_ref, z0_ref, z1_ref, w_ref, o_ref):
    z = jnp.concatenate([z0_ref[...], z1_ref[...]], axis=1)
    y = y_ref[...] * jax.nn.silu(z)
    ms = jnp.mean(y * y, axis=-1, keepdims=True)
    o_ref[...] = (y * lax.rsqrt(ms + EPS) * w_ref[...]).astype(o_ref.dtype)


def _ssd_norm(y, proj, norm_w, tm=256):
    m, d = y.shape
    half = d // 2
    zb = OFF_Z // half
    return pl.pallas_call(
        _ssd_norm_body,
        grid=(m // tm,),
        in_specs=[pl.BlockSpec((tm, d), lambda i: (i, 0)),
                  pl.BlockSpec((tm, half), lambda i: (i, zb)),
                  pl.BlockSpec((tm, half), lambda i: (i, zb + 1)),
                  pl.BlockSpec((1, d), lambda i: (0, 0))],
        out_specs=pl.BlockSpec((tm, d), lambda i: (i, 0)),
        out_shape=jax.ShapeDtypeStruct((m, d), BF16),
        compiler_params=_cparams(("arbitrary",)),
        name="ssd_norm",
    )(y, proj, proj, norm_w)


def _hyena_filters(seq, w1, b1, w2, b2, w3, freq):
    t = jnp.arange(seq, dtype=F32) / seq
    bands = jnp.linspace(1e-4, HY_BANDS - 1, HY_BANDS, dtype=F32)
    ang = 2.0 * math.pi * t[:, None] * bands[None, :]
    feat = jnp.concatenate([t[:, None], jnp.cos(ang), -jnp.sin(ang)], axis=-1)
    dot = functools.partial(jnp.dot, precision=HIGHEST)
    hdn = jnp.sin(freq[0] * (dot(feat, w1) + b1))
    hdn = jnp.sin(freq[1] * (dot(hdn, w2) + b2))
    filt = dot(hdn, w3).reshape(seq, 2, HY_ORDER, D_HY)
    deltas = jnp.abs(jnp.linspace(HY_MIN_DECAY, HY_MAX_DECAY, D_HY, dtype=F32))
    filt = filt * jnp.exp(-t[:, None] * deltas[None, :])[:, None, None, :]
    hf = filt[:, 0].reshape(seq, HY_ORDER * D_HY)
    hb = filt[:, 1].reshape(seq, HY_ORDER * D_HY)
    return hf + hb, hf - hb


def _split_bf16(x):
    hi = x.astype(BF16)
    return hi, (x - hi.astype(F32)).astype(BF16)


def _dft_ops(seq):
    n = 2 * seq
    f = jnp.arange(seq, dtype=jnp.int32)[:, None]
    t = jnp.arange(seq, dtype=jnp.int32)[None, :]
    ang = ((f * t) % n).astype(F32) * (math.pi / seq)
    cos_ft, sin_ft = jnp.cos(ang), jnp.sin(ang)
    nyq = jnp.where(t % 2 == 0, 1.0, -1.0).astype(F32)
    fwd = jnp.concatenate([cos_ft, nyq, -sin_ft[1:]], axis=0)
    wgt = jnp.where(f == 0, 1.0, 2.0).astype(F32) / n
    inv = jnp.concatenate([(wgt * cos_ft).T, nyq.T / n, (-(2.0 / n) * sin_ft[1:]).T], axis=1)
    return _split_bf16(fwd), _split_bf16(inv)


def _dot3(op_hi, op_lo, x):
    d = functools.partial(jnp.dot, preferred_element_type=F32)
    x_hi, x_lo = _split_bf16(x)
    return d(op_hi, x_hi) + (d(op_hi, x_lo) + d(op_lo, x_hi))


def _spectra_body(hs_ref, hd_ref, fh_ref, fl_ref, kr_ref, ki_ref, kn_ref, *, seq):
    fs = _dot3(fh_ref[...], fl_ref[...], hs_ref[...])
    fd = _dot3(fh_ref[...], fl_ref[...], hd_ref[...])
    row0 = lax.broadcasted_iota(jnp.int32, (seq, 1), 0) == 0
    kr_ref[...] = fs[:seq]
    kn_ref[...] = jnp.broadcast_to(fs[seq:seq + 1], kn_ref.shape)
    ki_ref[...] = jnp.where(row0, 0.0, fd[seq:])


def _hyena_spectra(hsum, hdiff, dft, seq, tc=256):
    n = hsum.shape[1]
    blk = pl.BlockSpec((seq, tc), lambda cb: (0, cb))
    const = pl.BlockSpec((2 * seq, seq), lambda cb: (0, 0), pipeline_mode=pl.Buffered(1))
    return pl.pallas_call(
        functools.partial(_spectra_body, seq=seq),
        grid=(n // tc,),
        in_specs=[blk, blk, const, const],
        out_specs=[blk, blk, pl.BlockSpec((8, tc), lambda cb: (0, cb))],
        out_shape=[jax.ShapeDtypeStruct((seq, n), F32)] * 2 + [jax.ShapeDtypeStruct((8, n), F32)],
        compiler_params=_cparams(("arbitrary",)),
        name="hyena_spectra",
    )(hsum, hdiff, *dft[0])


def _hyena_body(*refs, seq, seg, aliased):
    (v_ref, x1_ref, x2_ref, cwv_ref, cw1_ref, cw2_ref, cbv_ref, cb1_ref, cb2_ref,
     kr0_ref, kr1_ref, ki0_ref, ki1_ref, kn0_ref, kn1_ref, bias_ref, fh_ref, fl_ref, gh_ref, gl_ref) = refs[:20]
    o_ref = refs[-1]
    kr, ki, kn = (kr0_ref, kr1_ref), (ki0_ref, ki1_ref), (kn0_ref, kn1_ref)
    z = _conv3(v_ref[...], cwv_ref, cbv_ref, seg)
    gates = (_conv3(x1_ref[...], cw1_ref, cb1_ref, seg), _conv3(x2_ref[...], cw2_ref, cb2_ref, seg))
    row0 = lax.broadcasted_iota(jnp.int32, (seq, 1), 0) == 0
    for o in range(HY_ORDER):
        zf = _dot3(fh_ref[...], fl_ref[...], z)
        a, b = zf[:seq], zf[seq:]
        pre = a * kr[o][...] - b * ki[o][...]
        pim = a * ki[o][...] + b * jnp.where(row0, kn[o][0:1, :], kr[o][...])
        conv = _dot3(gh_ref[...], gl_ref[...], jnp.concatenate([pre, pim], axis=0))
        z = gates[o] * (conv + bias_ref[o:o + 1, :] * z)
    o_ref[...] = z.astype(o_ref.dtype)


def _hyena(proj, conv_w, conv_b, spectra, bias, dft, out_prev, *, n_rows, row0, n_b, seq, seg, tc):
    assert HY_ORDER == 2
    kr, ki, kn = spectra
    (fh, fl), (gh, gl) = dft
    rb0 = row0 // seq
    ncb = D_HY // tc
    assert PROJ_HY % tc == 0
    u_spec = lambda part: pl.BlockSpec((seq, tc), lambda cb, b: (rb0 + b, PROJ_HY // tc + part * ncb + cb))
    w_spec = lambda part: pl.BlockSpec((3, tc), lambda cb, b: (0, part * ncb + cb))
    b_spec = lambda part: pl.BlockSpec((1, tc), lambda cb, b: (0, part * ncb + cb))
    k_spec = lambda o: pl.BlockSpec((seq, tc), lambda cb, b: (0, o * ncb + cb), pipeline_mode=pl.Buffered(1))
    n_spec = lambda o: pl.BlockSpec((8, tc), lambda cb, b: (0, o * ncb + cb))
    const = lambda shape: pl.BlockSpec(shape, lambda cb, b: (0, 0), pipeline_mode=pl.Buffered(1))
    in_specs = [u_spec(0), u_spec(1), u_spec(2), w_spec(0), w_spec(1), w_spec(2), b_spec(0), b_spec(1), b_spec(2),
                k_spec(0), k_spec(1), k_spec(0), k_spec(1), n_spec(0), n_spec(1),
                pl.BlockSpec((HY_ORDER, tc), lambda cb, b: (0, cb)),
                const((2 * seq, seq)), const((2 * seq, seq)), const((seq, 2 * seq)), const((seq, 2 * seq))]
    args = [proj, proj, proj, conv_w, conv_w, conv_w, conv_b, conv_b, conv_b, kr, kr, ki, ki, kn, kn, bias,
            fh, fl, gh, gl]
    aliases = {}
    if out_prev is not None:
        in_specs.append(pl.BlockSpec(memory_space=pl.ANY))
        args.append(out_prev)
        aliases = {len(args) - 1: 0}
    return pl.pallas_call(
        functools.partial(_hyena_body, seq=seq, seg=seg, aliased=out_prev is not None),
        grid=(ncb, n_b),
        in_specs=in_specs,
        out_specs=pl.BlockSpec((seq, tc), lambda cb, b: (rb0 + b, cb)),
        out_shape=jax.ShapeDtypeStruct((n_rows, D_HY), BF16),
        input_output_aliases=aliases,
        compiler_params=_cparams(("arbitrary", "arbitrary")),
        name="hyena",
    )(*args)


def _s5_mixer(proj, p, ops, l, dims):
    n_cb, n_cs, n_lb, n_ls = dims
    t_ctx = n_cb * n_cs
    nq_c, nq_l = n_cs // S5_Q, n_ls // S5_Q
    u = lax.optimization_barrier(proj[:, :D_S5]).astype(BF16)

    def to_rows(a, nb, nq):
        a = jnp.transpose(a.reshape(nb, nq, S5_Q, S5_GROUPS, S5_H), (3, 1, 0, 2, 4))
        return a.reshape(S5_GROUPS, nq * nb, S5_Q * S5_H)

    def from_rows(a, nb, nq):
        a = jnp.transpose(a.reshape(S5_GROUPS, nq, nb, S5_Q, S5_H), (2, 1, 3, 0, 4))
        return a.reshape(nb * nq * S5_Q, D_S5)

    u_rows = jnp.concatenate([to_rows(u[:t_ctx], n_cb, nq_c), to_rows(u[t_ctx:], n_lb, nq_l)], axis=1)
    sre, sim = p['state_s5_re'][:, l], p['state_s5_im'][:, l]
    x0 = jnp.transpose(jnp.concatenate([sre[:, 0], sre[:, 1], sim[:, 0], sim[:, 1]], axis=-1), (1, 0, 2))
    y_rows, fin = _s5_scan(u_rows, ops, x0, n_cb, nq_c, n_lb, nq_l)
    y = jnp.concatenate([from_rows(y_rows[:, :nq_c * n_cb], n_cb, nq_c),
                         from_rows(y_rows[:, nq_c * n_cb:], n_lb, nq_l)], axis=0)
    y = _s5_glu(y, p['s5_w_glu'], l)
    fin = jnp.transpose(fin.reshape(S5_GROUPS, n_cb, 4, S5_P), (1, 2, 0, 3))
    return y, fin[:, 0:2], fin[:, 2:4]


def _ssd_mixer(proj, dt_raw, p, l, dims, fin_prev):
    n_cb, n_cs, n_lb, n_ls = dims
    t_ctx = n_cb * n_cs
    t_all = proj.shape[0]
    pad_heads = jnp.zeros((SSD_NGROUPS, LANES - 2 * SSD_HPG), F32)

    def per_group(a):
        return jnp.concatenate([a[0].reshape(SSD_NGROUPS, SSD_HPG), a[1].reshape(SSD_NGROUPS, SSD_HPG),
                                pad_heads], axis=1)[:, None, :]

    dtb_rows = per_group(p['ssd_dt_bias'][l])
    a_rows = per_group(-jnp.exp(p['ssd_a_log'][l]))
    d_rows = jnp.repeat(p['ssd_d'][l], SSD_HEADDIM).reshape(SSD_NGROUPS, 1, SSD_HPG * SSD_HEADDIM)
    dtr = dt_raw.reshape(t_all, 2, SSD_NGROUPS, SSD_HPG)
    dtg = jnp.concatenate([jnp.transpose(dtr, (2, 0, 1, 3)).reshape(SSD_NGROUPS, t_all, 2 * SSD_HPG),
                           jnp.zeros((SSD_NGROUPS, t_all, LANES - 2 * SSD_HPG), F32)], axis=-1)
    cw, cb = p['ssd_conv_w'][l], p['ssd_conv_b'][l][None]
    y, fin = _ssd_scan(proj, dtg, cw, cb, dtb_rows, a_rows, d_rows, None, None, fin_prev,
                       layer=l, row0=0, n_b=n_cb, seq=n_cs, seg=n_cs, want_fin=True)
    y, _ = _ssd_scan(proj, dtg, cw, cb, dtb_rows, a_rows, d_rows, p['state_ssd'][:, l], y, None,
                     layer=l, row0=t_ctx, n_b=n_lb, seq=n_ls, seg=GRID_W, want_fin=False)
    return _ssd_norm(y, proj, p['ssd_norm'][l][None]), fin


def _hyena_mixer(proj, p, l, dims, dfts):
    n_cb, n_cs, n_lb, n_ls = dims
    hcw, hcb = p['hy_conv_w'][l], p['hy_conv_b'][l][None]
    out = None
    cfgs = ((0, n_cb, n_cs, n_cs, 512), (n_cb * n_cs, n_lb, n_ls, GRID_W, 256))
    for (row0, nb, seq, seg, tc), dft in zip(cfgs, dfts):
        hsum, hdiff = _hyena_filters(seq, p['hy_w1'][l], p['hy_b1'][l], p['hy_w2'][l], p['hy_b2'][l], p['hy_w3'][l],
                                     p['hy_freq'][l])
        out = _hyena(proj, hcw, hcb, _hyena_spectra(hsum, hdiff, dft, seq), p['hy_bias'][l], dft, out,
                     n_rows=proj.shape[0], row0=row0, n_b=nb, seq=seq, seg=seg, tc=tc)
    return out


def _in_proj_body(a_ref, wt_ref, o_ref, wbf_ref):
    @pl.when(pl.program_id(1) == 0)
    def _():
        wbf_ref[...] = wt_ref[...].astype(BF16)

    o_ref[...] = lax.dot_general(a_ref[...], wbf_ref[...], (((1,), (1,)), ((), ())), preferred_element_type=F32)


def _in_proj(h, w_in, l, tm=1024):
    m, d = h.shape
    assert N_IN % 8 == 0 and OFF_HY % 8 == 0
    wt = jnp.swapaxes(w_in, 1, 2).reshape(w_in.shape[0] * N_IN, d)
    n_main, n_hy = OFF_DT // PROJ_TN, (N_IN - OFF_HY) // PROJ_TN
    assert OFF_DT % PROJ_TN == 0 and (N_IN - OFF_HY) % PROJ_TN == 0 and OFF_DT + PROJ_TN <= N_IN
    assert (n_main + n_hy + 1) * PROJ_TN == N_PROJ and m % tm == 0

    def w_row(j):
        return jnp.where(j < n_main, j * PROJ_TN, jnp.where(j < n_main + n_hy, OFF_HY + (j - n_main) * PROJ_TN, OFF_DT))

    return pl.pallas_call(
        _in_proj_body,
        grid=(N_PROJ // PROJ_TN, m // tm),
        in_specs=[pl.BlockSpec((tm, d), lambda j, i: (i, 0)),
                  pl.BlockSpec((pl.Element(PROJ_TN), pl.Element(d)),
                               lambda j, i: (pl.multiple_of(l * N_IN + w_row(j), 8), 0))],
        out_specs=pl.BlockSpec((tm, PROJ_TN), lambda j, i: (i, j)),
        out_shape=jax.ShapeDtypeStruct((m, N_PROJ), F32),
        scratch_shapes=[pltpu.VMEM((PROJ_TN, d), BF16)],
        compiler_params=_cparams(("arbitrary", "arbitrary")),
        name="w_in",
    )(h, wt)


def _moe(h_packed, route, wg, wu, wd, lead):
    t_all = h_packed.shape[0]
    token_of_row, row_of_pair, item_e, item_b, item_rows = _moe_dispatch(route, t_all, FFN_MOE_ROWS[-1])
    xs = jnp.take(h_packed, token_of_row, axis=0, mode="clip")
    ys = _ffn(xs, wg, wu, wd, lead, item_e, item_b, item_rows, packed=True, row_variants=FFN_MOE_ROWS)
    return (jnp.take(ys, row_of_pair[:, 0], axis=0, mode="clip"),
            jnp.take(ys, row_of_pair[:, 1], axis=0, mode="clip"))


def kernel(x_prompt, x_sample, state_s5_re, state_s5_im, state_ssd, c, c_ctx, ada_w, ada_b, norm1, norm2, final_norm, w_in, w_out, s5_a_re, s5_a_im, s5_log_dt, s5_b_re, s5_b_im, s5_c_re, s5_c_im, s5_d, s5_w_glu, ssd_conv_w, ssd_conv_b, ssd_dt_bias, ssd_a_log, ssd_d, ssd_norm, hy_conv_w, hy_conv_b, hy_w1, hy_b1, hy_w2, hy_b2, hy_w3, hy_freq, hy_bias, ffn_wg, ffn_wu, ffn_wd, moe_router, moe_wg, moe_wu, moe_wd):
    n_cb, n_cs, d = x_prompt.shape
    n_lb, n_ls, _ = x_sample.shape
    assert d == D_MODEL and n_ls == LAT_ROWS and n_cs % SSD_CHUNK == 0 and n_ls % SSD_CHUNK == 0
    t_ctx, t_lat = n_cb * n_cs, n_lb * n_ls
    t_all = t_ctx + t_lat
    assert t_ctx % LAT_ROWS == 0 and t_all % FFN_ROWS == 0 and 1 + n_lb <= 3
    x = jnp.concatenate([x_prompt.reshape(t_ctx, d), x_sample.reshape(t_lat, d)], axis=0)

    cond = jnp.zeros((8, d), F32).at[0].set(c_ctx).at[1:1 + n_lb].set(c)
    mod = jnp.concatenate(
        [_matmul(cond, ada_w, l, 6 * d, tm=8, tn=512, out_dtype=F32, a_silu=True,
                 bias=ada_b.reshape(DEPTH, 1, 6 * d), name="ada_mod")[None] for l in range(DEPTH)], axis=0)
    mod3 = mod[:, :3].reshape(DEPTH, 3, 1, 6 * d)

    p = dict(state_s5_re=state_s5_re, state_s5_im=state_s5_im, state_ssd=state_ssd,
             s5_a_re=s5_a_re, s5_a_im=s5_a_im, s5_log_dt=s5_log_dt, s5_b_re=s5_b_re, s5_b_im=s5_b_im,
             s5_c_re=s5_c_re, s5_c_im=s5_c_im, s5_d=s5_d, s5_w_glu=s5_w_glu,
             ssd_conv_w=ssd_conv_w, ssd_conv_b=ssd_conv_b, ssd_dt_bias=ssd_dt_bias, ssd_a_log=ssd_a_log,
             ssd_d=ssd_d, ssd_norm=ssd_norm, hy_conv_w=hy_conv_w, hy_conv_b=hy_conv_b, hy_w1=hy_w1, hy_b1=hy_b1,
             hy_w2=hy_w2, hy_b2=hy_b2, hy_w3=hy_w3, hy_freq=hy_freq, hy_bias=hy_bias)
    dims = (n_cb, n_cs, n_lb, n_ls)
    dfts = (_dft_ops(n_cs), _dft_ops(n_ls))
    s5_ops = jax.vmap(_s5_prep)(s5_a_re, s5_a_im, s5_log_dt, s5_b_re, s5_b_im, s5_c_re, s5_c_im, s5_d)
    fin_re, fin_im, fin_ssd = [], [], None
    for l in range(DEPTH):
        m3 = mod3[l]
        h = _norm_mod(x, norm1[l][None], m3, 1, 0, t_ctx)
        proj = _in_proj(h, w_in, l)
        dt_raw = proj[:, PROJ_DT:PROJ_DT + 2 * SSD_HEADS]
        y_s5, f_re, f_im = _s5_mixer(proj, p, tuple(o[l] for o in s5_ops), l, dims)
        y_ssd, fin_ssd = _ssd_mixer(proj, dt_raw, p, l, dims, fin_ssd)
        y_hy = _hyena_mixer(proj, p, l, dims, dfts)
        fin_re.append(f_re)
        fin_im.append(f_im)
        x = _matmul([y_s5, y_ssd, y_hy], w_out, l, d, tm=1024, tn=512, out_dtype=F32, resid=x,
                    gate=m3[:, :, 2 * d:3 * d], n_ctx_rows=t_ctx, name="w_out")
        if l % 2 == 0:
            h = _norm_mod(x, norm2[l][None], m3, 4, 3, t_ctx)
            n_items = t_all // FFN_ROWS
            fs = [_ffn(h, ffn_wg, ffn_wu, ffn_wd, l // 2, jnp.zeros((n_items,), jnp.int32),
                       jnp.arange(n_items, dtype=jnp.int32), jnp.full((n_items,), FFN_ROWS, jnp.int32),
                       packed=False, row_variants=FFN_DENSE_ROWS)]
            route = None
        else:
            router = jnp.concatenate([moe_router[l // 2], jnp.zeros((d, LANES - N_EXPERTS), F32)], axis=1)
            h_packed, route = _norm_mod(x, norm2[l][None], m3, 4, 3, t_ctx, router=router)
            fs = _moe(h_packed, route, moe_wg, moe_wu, moe_wd, l // 2)
        if l < DEPTH - 1:
            x = _resid(x, fs, m3, 5, t_ctx, route=route)
        else:
            y_ctx, y_lat = (_resid(x, fs, m3, 5, t_ctx, route=route, final_w=final_norm[None], rows=rows)
                            for rows in ((0, t_ctx), (t_ctx, t_lat)))

    y_prompt = y_ctx.reshape(n_cb, n_cs, d)
    y_sample = y_lat.reshape(n_lb, n_ls, d)
    return (y_prompt, y_sample, jnp.stack(fin_re, axis=1), jnp.stack(fin_im, axis=1), fin_ssd)
```
